```python
import math
import jax, jax.numpy as jnp
from jax import lax
import numpy as np

D_MODEL = 1024
BATCH = 16
SEQ = 2048
DEPTH = 2

EPS = 1e-6
PLE_DIM = 256
D_FF = 2816
CONV_W = 4
HEAD_DIM = 64
LRU_WIDTH = D_MODEL // 4
LRU_BLOCKS = LRU_WIDTH // HEAD_DIM
LRU_BLOCK = LRU_WIDTH // LRU_BLOCKS
LRU_C = 8.0
ATT_WIDTH = D_MODEL // 2
ATT_HEADS = ATT_WIDTH // HEAD_DIM
ATT_KV_HEADS = 2
ATT_GROUP = ATT_HEADS // ATT_KV_HEADS
KV_WIDTH = ATT_KV_HEADS * HEAD_DIM
WINDOW = 128
BLOCK_Q = 128
REL_BUCKETS = 32
REL_MAX_DIST = 128
DN_WIDTH = D_MODEL // 4
DN_HEADS = DN_WIDTH // HEAD_DIM
DN_DK = HEAD_DIM
DN_DV = HEAD_DIM
DN_QK = DN_HEADS * DN_DK
DN_CHUNK = 64
D_MIX = LRU_WIDTH + ATT_WIDTH + DN_WIDTH
IN_SPLITS = (LRU_WIDTH, LRU_WIDTH,
             ATT_WIDTH, KV_WIDTH, KV_WIDTH,
             DN_QK, DN_QK, DN_WIDTH, DN_WIDTH,
             DN_HEADS, DN_HEADS)
D_IN = sum(IN_SPLITS)

kernel_name = "hymba_style_lru_swa_deltanet_macaron"


def rms_norm(x, g):
    xf = x.astype(jnp.float32)
    y = xf * lax.rsqrt(jnp.mean(xf * xf, axis=-1, keepdims=True) + EPS)
    return (y * g.astype(jnp.float32)).astype(x.dtype)


def swiglu(x, w_gate, w_up, w_down):
    return (jax.nn.silu(x @ w_gate) * (x @ w_up)) @ w_down


def causal_dwconv(x, w, b=None):
    K = w.shape[0]
    S = x.shape[1]
    xp = jnp.pad(x, ((0, 0), (K - 1, 0), (0, 0)))
    y = xp[:, 0:S] * w[0]
    for k in range(1, K):
        y = y + xp[:, k:k + S] * w[k]
    if b is not None:
        y = y + b
    return y


def split_points():
    return np.cumsum(np.array(IN_SPLITS))[:-1].tolist()


def rg_lru(x, w_a, b_a, w_x, b_x, lam):
    B, S, _ = x.shape
    xb = x.reshape(B, S, LRU_BLOCKS, LRU_BLOCK)
    r = jax.nn.sigmoid(jnp.einsum('bshi,hij->bshj', xb, w_a).reshape(B, S, LRU_WIDTH) + b_a)
    i = jax.nn.sigmoid(jnp.einsum('bshi,hij->bshj', xb, w_x).reshape(B, S, LRU_WIDTH) + b_x)
    log_a = -LRU_C * r.astype(jnp.float32) * jax.nn.softplus(-lam.astype(jnp.float32))
    a = jnp.exp(log_a)
    u = jnp.sqrt(-jnp.expm1(2.0 * log_a)) * (i * x).astype(jnp.float32)

    def combine(left, right):
        a1, b1 = left
        a2, b2 = right
        return a1 * a2, a2 * b1 + b2

    _, h = lax.associative_scan(combine, (a, u), axis=1)
    return h.astype(x.dtype)


def rel_bucket(dist):
    max_exact = REL_BUCKETS // 2
    large = max_exact + (jnp.log(jnp.maximum(dist, 1).astype(jnp.float32) / max_exact)
                         / math.log(REL_MAX_DIST / max_exact)
                         * (REL_BUCKETS - max_exact)).astype(jnp.int32)
    large = jnp.minimum(large, REL_BUCKETS - 1)
    return jnp.where(dist < max_exact, dist, large)


def swa_attention(q, k, v, sinks, rel_bias):
    B, S = q.shape[:2]
    NB = S // BLOCK_Q
    qb = q.reshape(B, NB, BLOCK_Q, ATT_KV_HEADS, ATT_GROUP, HEAD_DIM)

    def with_prev(t):
        tb = t.reshape(B, NB, BLOCK_Q, ATT_KV_HEADS, HEAD_DIM)
        prev = jnp.pad(tb, ((0, 0), (1, 0), (0, 0), (0, 0), (0, 0)))[:, :-1]
        return jnp.concatenate([prev, tb], axis=2)

    kb, vb = with_prev(k), with_prev(v)
    qi = jnp.arange(BLOCK_Q)[:, None]
    kj = jnp.arange(2 * BLOCK_Q)[None, :]
    dist = BLOCK_Q + qi - kj
    band = (dist >= 0) & (dist < WINDOW)
    blk = jnp.arange(NB)[:, None, None]
    valid = band[None] & ((blk > 0) | (kj[None] >= BLOCK_Q))
    bias = rel_bias.astype(jnp.float32)[rel_bucket(jnp.maximum(dist, 0))]
    bias = bias.transpose(2, 0, 1).reshape(ATT_KV_HEADS, ATT_GROUP, BLOCK_Q, 2 * BLOCK_Q)
    s = jnp.einsum('bnikgd,bnjkd->bnkgij', qb, kb).astype(jnp.float32) * (HEAD_DIM ** -0.5) + bias
    s = jnp.where(valid[None, :, None, None], s, -jnp.inf)
    sink = sinks.astype(jnp.float32).reshape(ATT_KV_HEADS, ATT_GROUP)[:, :, None, None]
    m = jnp.maximum(jnp.max(s, axis=-1, keepdims=True), sink)
    e = jnp.exp(s - m)
    probs = e / (jnp.sum(e, axis=-1, keepdims=True) + jnp.exp(sink - m))
    o = jnp.einsum('bnkgij,bnjkd->bnikgd', probs.astype(v.dtype), vb)
    return o.reshape(B, S, ATT_WIDTH)


def l2norm(t):
    return t * lax.rsqrt(jnp.sum(t * t, axis=-1, keepdims=True) + EPS)


def gated_delta_rule(q, k, v, g, beta):
    B, S, H, DK = k.shape
    DV = v.shape[-1]
    C = DN_CHUNK
    NC = S // C
    f32 = jnp.float32
    q = l2norm(q.astype(f32)) * (DK ** -0.5)
    k = l2norm(k.astype(f32))

    def chunks(t):
        return t.reshape(B, NC, C, H, -1).transpose(1, 0, 3, 2, 4)

    qc, kc, vc = chunks(q), chunks(k), chunks(v.astype(f32))
    gc = g.astype(f32).reshape(B, NC, C, H).transpose(1, 0, 3, 2)
    bc = beta.astype(f32).reshape(B, NC, C, H).transpose(1, 0, 3, 2)
    gcum = jnp.cumsum(gc, axis=-1)
    tril = jnp.tril(jnp.ones((C, C), dtype=bool))
    strict = jnp.tril(jnp.ones((C, C), dtype=bool), -1)
    decay = jnp.exp(jnp.where(tril, gcum[..., :, None] - gcum[..., None, :], -jnp.inf))
    k_beta = kc * bc[..., None]
    v_beta = vc * bc[..., None]
    Lmat = jnp.where(strict, jnp.einsum('...id,...jd->...ij', k_beta, kc) * decay, 0.0)
    eye = jnp.eye(C, dtype=f32)
    T = lax.linalg.triangular_solve(Lmat + eye, jnp.broadcast_to(eye, Lmat.shape),
                                    left_side=True, lower=True, unit_diagonal=True)
    u = jnp.einsum('...ij,...jd->...id', T, v_beta)
    w = jnp.einsum('...ij,...jd->...id', T, k_beta * jnp.exp(gcum)[..., None])

    def step(state, xs):
        q_i, k_i, u_i, w_i, g_i, dec_i = xs
        attn = jnp.einsum('bhid,bhjd->bhij', q_i, k_i) * dec_i
        v_new = u_i - jnp.einsum('bhcd,bhde->bhce', w_i, state)
        o = (jnp.einsum('bhcd,bhde->bhce', q_i * jnp.exp(g_i)[..., None], state)
             + jnp.einsum('bhij,bhje->bhie', attn, v_new))
        g_last = g_i[..., -1]
        k_dec = k_i * jnp.exp(g_last[..., None] - g_i)[..., None]
        state = state * jnp.exp(g_last)[..., None, None] + jnp.einsum('bhcd,bhce->bhde', k_dec, v_new)
        return state, o

    state0 = jnp.zeros((B, H, DK, DV), f32)
    _, o = lax.scan(step, state0, (qc, kc, u, w, gcum, decay))
    return o.transpose(1, 0, 3, 2, 4).reshape(B, S, H, DV)


def hybrid_mixer(xn, w_in, lru_conv_w, lru_conv_b, lru_w_a, lru_b_a, lru_w_x, lru_b_x, lru_lambda,
                 attn_sinks, rel_bias, dn_conv_w, dn_a_log, dn_dt_bias, dn_norm, w_out):
    B, S, _ = xn.shape
    u = xn @ w_in
    (lru_x, lru_gate, att_q, att_k, att_v, dn_q, dn_k, dn_v, dn_z, dn_b, dn_a) = jnp.split(
        u, split_points(), axis=-1)
    xr = causal_dwconv(lru_x, lru_conv_w, lru_conv_b)
    y_lru = jax.nn.gelu(lru_gate) * rg_lru(xr, lru_w_a, lru_b_a, lru_w_x, lru_b_x, lru_lambda)
    y_att = swa_attention(att_q.reshape(B, S, ATT_HEADS, HEAD_DIM),
                          att_k.reshape(B, S, ATT_KV_HEADS, HEAD_DIM),
                          att_v.reshape(B, S, ATT_KV_HEADS, HEAD_DIM),
                          attn_sinks, rel_bias)
    qkv = jax.nn.silu(causal_dwconv(jnp.concatenate([dn_q, dn_k, dn_v], axis=-1), dn_conv_w))
    q, k, v = jnp.split(qkv, [DN_QK, 2 * DN_QK], axis=-1)
    beta = jax.nn.sigmoid(dn_b.astype(jnp.float32))
    g = -jnp.exp(dn_a_log.astype(jnp.float32)) * jax.nn.softplus(
        dn_a.astype(jnp.float32) + dn_dt_bias.astype(jnp.float32))
    o = gated_delta_rule(q.reshape(B, S, DN_HEADS, DN_DK), k.reshape(B, S, DN_HEADS, DN_DK),
                         v.reshape(B, S, DN_HEADS, DN_DV), g, beta)
    z = dn_z.reshape(B, S, DN_HEADS, DN_DV).astype(jnp.float32)
    o = (o * lax.rsqrt(jnp.mean(o * o, axis=-1, keepdims=True) + EPS)
         * dn_norm.astype(jnp.float32) * jax.nn.silu(z))
    y_dn = o.reshape(B, S, DN_WIDTH).astype(xn.dtype)
    return jnp.concatenate([y_lru, y_att, y_dn], axis=-1) @ w_out


def setup_inputs(seed: int = 0) -> dict:
    key = jax.random.key(seed)
    ks = list(jax.random.split(key, 48))

    def nrm(shape, scale):
        return scale * jax.random.normal(ks.pop(), shape, jnp.float32)

    def gain(shape):
        return 1.0 + 0.02 * jax.random.normal(ks.pop(), shape, jnp.float32)

    L, D = DEPTH, D_MODEL
    x = nrm((BATCH, SEQ, D), 1.0)
    p = nrm((DEPTH, BATCH, SEQ, PLE_DIM), 1.0)
    a_c = jax.random.uniform(ks.pop(), (L, LRU_WIDTH), jnp.float32, 0.9, 0.999)
    s = a_c ** (1.0 / LRU_C)
    lru_lambda = jnp.log(s) - jnp.log1p(-s)
    dn_a_log = jnp.log(jax.random.uniform(ks.pop(), (L, DN_HEADS), jnp.float32, 1.0, 16.0))
    dt = jnp.exp(jax.random.uniform(ks.pop(), (L, DN_HEADS), jnp.float32,
                                    math.log(1e-3), math.log(1e-1)))
    dn_dt_bias = dt + jnp.log(-jnp.expm1(-dt))
    return {
        "x": x,
        "p": p,
        "ffn1_norm": gain((L, D)),
        "ffn1_w_gate": nrm((L, D, D_FF), D ** -0.5),
        "ffn1_w_up": nrm((L, D, D_FF), D ** -0.5),
        "ffn1_w_down": nrm((L, D_FF, D), D_FF ** -0.5),
        "mix_norm": gain((L, D)),
        "w_in": nrm((L, D, D_IN), D ** -0.5),
        "lru_conv_w": nrm((L, CONV_W, LRU_WIDTH), CONV_W ** -0.5),
        "lru_conv_b": nrm((L, LRU_WIDTH), 0.01),
        "lru_w_a": nrm((L, LRU_BLOCKS, LRU_BLOCK, LRU_BLOCK), LRU_BLOCK ** -0.5),
        "lru_b_a": nrm((L, LRU_WIDTH), 0.01),
        "lru_w_x": nrm((L, LRU_BLOCKS, LRU_BLOCK, LRU_BLOCK), LRU_BLOCK ** -0.5),
        "lru_b_x": nrm((L, LRU_WIDTH), 0.01),
        "lru_lambda": lru_lambda,
        "attn_sinks": nrm((L, ATT_HEADS), 0.5),
        "rel_bias": nrm((REL_BUCKETS, ATT_HEADS), 0.5),
        "dn_conv_w": nrm((L, CONV_W, 2 * DN_QK + DN_WIDTH), CONV_W ** -0.5),
        "dn_a_log": dn_a_log,
        "dn_dt_bias": dn_dt_bias,
        "dn_norm": gain((L, DN_DV)),
        "w_out": nrm((L, D_MIX, D), D_MIX ** -0.5),
        "ffn2_norm": gain((L, D)),
        "ffn2_w_gate": nrm((L, D, D_FF), D ** -0.5),
        "ffn2_w_up": nrm((L, D, D_FF), D ** -0.5),
        "ffn2_w_down": nrm((L, D_FF, D), D_FF ** -0.5),
        "ple_norm": gain((L, D)),
        "ple_w_gate": nrm((L, D, D), D ** -0.5),
        "ple_w_proj": nrm((L, PLE_DIM, D), PLE_DIM ** -0.5),
        "final_norm": gain((D,)),
    }


def reference(x, p, ffn1_norm, ffn1_w_gate, ffn1_w_up, ffn1_w_down, mix_norm, w_in,
              lru_conv_w, lru_conv_b, lru_w_a, lru_b_a, lru_w_x, lru_b_x, lru_lambda,
              attn_sinks, rel_bias, dn_conv_w, dn_a_log, dn_dt_bias, dn_norm, w_out,
              ffn2_norm, ffn2_w_gate, ffn2_w_up, ffn2_w_down, ple_norm, ple_w_gate, ple_w_proj,
              final_norm):
    h = x
    for l in range(DEPTH):
        h = h + 0.5 * swiglu(rms_norm(h, ffn1_norm[l]), ffn1_w_gate[l], ffn1_w_up[l], ffn1_w_down[l])
        h = h + hybrid_mixer(rms_norm(h, mix_norm[l]), w_in[l],
                             lru_conv_w[l], lru_conv_b[l], lru_w_a[l], lru_b_a[l],
                             lru_w_x[l], lru_b_x[l], lru_lambda[l],
                             attn_sinks[l], rel_bias,
                             dn_conv_w[l], dn_a_log[l], dn_dt_bias[l], dn_norm[l], w_out[l])
        h = h + 0.5 * swiglu(rms_norm(h, ffn2_norm[l]), ffn2_w_gate[l], ffn2_w_up[l], ffn2_w_down[l])
        gate = jax.nn.sigmoid(rms_norm(h, ple_norm[l]) @ ple_w_gate[l])
        h = h + gate * (p[l] @ ple_w_proj[l])
    return rms_norm(h, final_norm)
```

```python
import functools
import math

import jax
import jax.numpy as jnp
from jax import lax
from jax.experimental import pallas as pl
from jax.experimental.pallas import tpu as pltpu

F32 = jnp.float32
BF16 = jnp.bfloat16

EPS = 1e-6
HEAD_DIM = 64
LRU_C = 8.0
CONV_W = 4
ATT_GROUP = 4
BLOCK_Q = 128
REL_BUCKETS = 32
REL_MAX_DIST = 128
DN_CHUNK = 64
DN_HEADS = 4
SUBLANES = 8
LANES = 128
MXU_DIM = 256
VMEM_LIMIT = 56 * 1024 * 1024

TOKEN_TILE = 512
FFN_CHUNK = MXU_DIM
LRU_TILE = 256
DN_TILE = 256


def _cparams(*sem):
    return pltpu.CompilerParams(dimension_semantics=sem, vmem_limit_bytes=VMEM_LIMIT)


def _rms(x, g):
    ms = jnp.mean(x * x, axis=-1, keepdims=True)
    return x * lax.rsqrt(ms + EPS) * g


def _mm(a, b):
    return jnp.dot(a, b, preferred_element_type=F32)


def _mm_nt(a, b):
    return lax.dot_general(a, b, (((1,), (1,)), ((), ())), preferred_element_type=F32)


def _mm_tn(a, b):
    return lax.dot_general(a, b, (((0,), (0,)), ((), ())), preferred_element_type=F32)


def _softplus(x):
    return jnp.maximum(x, 0.0) + jnp.log1p(jnp.exp(-jnp.abs(x)))


def _ffn_body(h_ref, g_ref, wg_ref, wu_ref, wd_ref, o_ref):
    h = h_ref[...]
    xn = _rms(h, g_ref[...]).astype(BF16)
    d_ff = wg_ref.shape[1]
    acc = jnp.zeros(h.shape, F32)
    for c in range(d_ff // FFN_CHUNK):
        cols = slice(c * FFN_CHUNK, (c + 1) * FFN_CHUNK)
        gate = _mm(xn, wg_ref[:, cols])
        up = _mm(xn, wu_ref[:, cols])
        act = (gate * jax.nn.sigmoid(gate) * up).astype(BF16)
        acc = acc + _mm(act, wd_ref[cols, :])
    o_ref[...] = h + 0.5 * acc


def _ffn(h, gain, wg, wu, wd):
    n, d = h.shape
    d_ff = wg.shape[1]
    full = lambda i: (0, 0)
    return pl.pallas_call(
        _ffn_body,
        grid=(n // TOKEN_TILE,),
        in_specs=[
            pl.BlockSpec((TOKEN_TILE, d), lambda i: (i, 0)),
            pl.BlockSpec((1, d), full),
            pl.BlockSpec((d, d_ff), full),
            pl.BlockSpec((d, d_ff), full),
            pl.BlockSpec((d_ff, d), full),
        ],
        out_specs=pl.BlockSpec((TOKEN_TILE, d), lambda i: (i, 0)),
        out_shape=jax.ShapeDtypeStruct((n, d), F32),
        compiler_params=_cparams("arbitrary"),
        name="ffn",
    )(h, gain, wg, wu, wd)


def _proj_in_body(h_ref, g_ref, w_ref, *o_refs):
    xn = _rms(h_ref[...], g_ref[...]).astype(BF16)
    off = 0
    for o_ref in o_refs:
        width = o_ref.shape[1]
        o_ref[...] = _mm(xn, w_ref[:, off:off + width]).astype(o_ref.dtype)
        off += width


def _proj_in(h, gain, w, widths):
    n, d = h.shape
    return pl.pallas_call(
        _proj_in_body,
        grid=(n // TOKEN_TILE,),
        in_specs=[
            pl.BlockSpec((TOKEN_TILE, d), lambda i: (i, 0)),
            pl.BlockSpec((1, d), lambda i: (0, 0)),
            pl.BlockSpec(w.shape, lambda i: (0, 0)),
        ],
        out_specs=[pl.BlockSpec((TOKEN_TILE, wd), lambda i: (i, 0)) for wd in widths],
        out_shape=[jax.ShapeDtypeStruct((n, wd), F32) for wd in widths],
        compiler_params=_cparams("arbitrary"),
        name="proj_in",
    )(h, gain, w)


def _proj_out_body(h_ref, ya_ref, yb_ref, yc_ref, w_ref, o_ref):
    y = jnp.concatenate([ya_ref[...], yb_ref[...], yc_ref[...]], axis=1).astype(BF16)
    o_ref[...] = h_ref[...] + _mm(y, w_ref[...])


def _proj_out(h, ya, yb, yc, w):
    n, d = h.shape
    row = lambda i: (i, 0)
    return pl.pallas_call(
        _proj_out_body,
        grid=(n // TOKEN_TILE,),
        in_specs=[
            pl.BlockSpec((TOKEN_TILE, d), row),
            pl.BlockSpec((TOKEN_TILE, ya.shape[1]), row),
            pl.BlockSpec((TOKEN_TILE, yb.shape[1]), row),
            pl.BlockSpec((TOKEN_TILE, yc.shape[1]), row),
            pl.BlockSpec(w.shape, lambda i: (0, 0)),
        ],
        out_specs=pl.BlockSpec((TOKEN_TILE, d), row),
        out_shape=jax.ShapeDtypeStruct((n, d), F32),
        compiler_params=_cparams("arbitrary"),
        name="proj_out",
    )(h, ya, yb, yc, w)


def _ple_body(h_ref, p_ref, g_ref, wg_ref, wp_ref, fg_ref, o_ref, *, final):
    h = h_ref[...]
    xn = _rms(h, g_ref[...]).astype(BF16)
    gate = jax.nn.sigmoid(_mm(xn, wg_ref[...]))
    out = h + gate * _mm(p_ref[...].astype(BF16), wp_ref[...])
    if final:
        out = _rms(out, fg_ref[...])
    o_ref[...] = out


def _ple(h, p, gain, wg, wp, final_gain, final):
    n, d = h.shape
    row = lambda i: (i, 0)
    full = lambda i: (0, 0)
    return pl.pallas_call(
        functools.partial(_ple_body, final=final),
        grid=(n // TOKEN_TILE,),
        in_specs=[
            pl.BlockSpec((TOKEN_TILE, d), row),
            pl.BlockSpec((TOKEN_TILE, p.shape[1]), row),
            pl.BlockSpec((1, d), full),
            pl.BlockSpec(wg.shape, full),
            pl.BlockSpec(wp.shape, full),
            pl.BlockSpec((1, d), full),
        ],
        out_specs=pl.BlockSpec((TOKEN_TILE, d), row),
        out_shape=jax.ShapeDtypeStruct((n, d), F32),
        compiler_params=_cparams("arbitrary"),
        name="ple",
    )(h, p, gain, wg, wp, final_gain)


def _causal_conv(x, tail_ref, w):
    rows = x.shape[0]
    ext = jnp.concatenate([tail_ref[...], x], axis=0)
    y = x * w[CONV_W - 1:CONV_W]
    for back in range(1, CONV_W):
        shifted = pltpu.roll(ext, back, 0)[SUBLANES:]
        y = y + shifted * w[CONV_W - 1 - back:CONV_W - back]
    tail_ref[...] = x[rows - SUBLANES:]
    return y


def _lru_body(u_ref, cw_ref, cb_ref, wax_ref, bax_ref, lam_ref, o_ref, tail_ref, h_ref):
    width = o_ref.shape[2]
    rows = o_ref.shape[1]

    @pl.when(pl.program_id(1) == 0)
    def _():
        tail_ref[...] = jnp.zeros_like(tail_ref)
        h_ref[...] = jnp.zeros_like(h_ref)

    u = u_ref[0]
    xr = _causal_conv(u[:, :width], tail_ref, cw_ref[...]) + cb_ref[...]
    gates = jax.nn.sigmoid(_mm(xr.astype(BF16), wax_ref[...]) + bax_ref[...])
    r, i = gates[:, :width], gates[:, width:]
    log_a = -LRU_C * r * _softplus(-lam_ref[...])
    a = jnp.exp(log_a)
    b = jnp.sqrt(-jnp.tanh(log_a) * (a * a + 1.0)) * (i * xr)

    sub = lax.broadcasted_iota(jnp.int32, a.shape, 0) % SUBLANES
    for d in (1, 2, 4):
        ok = sub >= d
        b = jnp.where(ok, a * pltpu.roll(b, d, 0) + b, b)
        a = jnp.where(ok, a * pltpu.roll(a, d, 0), a)
    carry = h_ref[...]
    outs = []
    for g in range(rows // SUBLANES):
        rs = slice(g * SUBLANES, (g + 1) * SUBLANES)
        hg = a[rs] * carry + b[rs]
        outs.append(hg)
        carry = hg[SUBLANES - 1:SUBLANES]
    h_ref[...] = carry
    hs = jnp.concatenate(outs, axis=0)
    o_ref[0] = (jax.nn.gelu(u[:, width:]) * hs).astype(o_ref.dtype)


def _lru(u, conv_w, conv_b, w_ax, b_ax, lam):
    bsz, seq, two_w = u.shape
    width = two_w // 2
    full = lambda b, t: (0, 0)
    return pl.pallas_call(
        _lru_body,
        grid=(bsz, seq // LRU_TILE),
        in_specs=[
            pl.BlockSpec((1, LRU_TILE, two_w), lambda b, t: (b, t, 0)),
            pl.BlockSpec(conv_w.shape, full),
            pl.BlockSpec(conv_b.shape, full),
            pl.BlockSpec(w_ax.shape, full),
            pl.BlockSpec(b_ax.shape, full),
            pl.BlockSpec(lam.shape, full),
        ],
        out_specs=pl.BlockSpec((1, LRU_TILE, width), lambda b, t: (b, t, 0)),
        out_shape=jax.ShapeDtypeStruct((bsz, seq, width), F32),
        scratch_shapes=[pltpu.VMEM((SUBLANES, width), F32), pltpu.VMEM((1, width), F32)],
        compiler_params=_cparams("arbitrary", "arbitrary"),
        name="rg_lru",
    )(u, conv_w, conv_b, w_ax, b_ax, lam)


def _attn_body(bucket_ref, relb_ref, sink_ref, q_ref, kvp_ref, kvc_ref, o_ref, bias_ref):
    n_kv = kvc_ref.shape[2] // (2 * HEAD_DIM)
    blk = pl.program_id(1)
    rows = ATT_GROUP * BLOCK_Q

    @pl.when((pl.program_id(0) == 0) & (blk == 0))
    def _():
        bucket = bucket_ref[...]
        for head in range(n_kv * ATT_GROUP):
            tbl = jnp.full(bucket.shape, -jnp.inf, F32)
            for bk in range(REL_BUCKETS):
                tbl = jnp.where(bucket == bk, relb_ref[bk, head], tbl)
            g, j = divmod(head, ATT_GROUP)
            bias_ref[g, j * BLOCK_Q:(j + 1) * BLOCK_Q, :] = tbl

    q = q_ref[0] * (HEAD_DIM ** -0.5)
    kv = jnp.concatenate([kvp_ref[0], kvc_ref[0]], axis=0)
    kw = n_kv * HEAD_DIM
    k_all, v_all = kv[:, :kw], kv[:, kw:]
    lane = lax.broadcasted_iota(jnp.int32, (2 * BLOCK_Q, LANES), 1)
    qlane = lax.broadcasted_iota(jnp.int32, (BLOCK_Q, LANES), 1)
    col = lax.broadcasted_iota(jnp.int32, (rows, 2 * BLOCK_Q), 1)
    first = jnp.logical_and(blk == 0, col < BLOCK_Q)

    for g in range(n_kv):
        pair = g // 2
        k2 = k_all[:, pair * LANES:(pair + 1) * LANES]
        v2 = v_all[:, pair * LANES:(pair + 1) * LANES]
        lo = (g % 2) == 0
        k_rot, v_rot = pltpu.roll(k2, HEAD_DIM, 1), pltpu.roll(v2, HEAD_DIM, 1)
        keep = (lane < HEAD_DIM) if lo else (lane >= HEAD_DIM)
        kd = jnp.where(keep, k2, k_rot).astype(BF16)
        vd = jnp.where(keep, v2, v_rot).astype(BF16)
        parts = []
        for pr in range(ATT_GROUP // 2):
            qp = q[:, (g * ATT_GROUP // 2 + pr) * LANES:(g * ATT_GROUP // 2 + pr + 1) * LANES]
            parts.append(jnp.where(qlane < HEAD_DIM, qp, 0.0))
            parts.append(jnp.where(qlane >= HEAD_DIM, qp, 0.0))
        lhs = jnp.concatenate(parts, axis=0).astype(BF16)
        s = _mm_nt(lhs, kd) + bias_ref[g]
        s = jnp.where(first, -jnp.inf, s)
        sink = sink_ref[g]
        m = jnp.maximum(jnp.max(s, axis=1, keepdims=True), sink)
        e = jnp.exp(s - m)
        den = jnp.sum(e, axis=1, keepdims=True) + jnp.exp(sink - m)
        o = _mm(e.astype(BF16), vd) / den
        for pr in range(ATT_GROUP // 2):
            oa = o[(2 * pr) * BLOCK_Q:(2 * pr + 1) * BLOCK_Q]
            ob = o[(2 * pr + 1) * BLOCK_Q:(2 * pr + 2) * BLOCK_Q]
            c0 = (g * ATT_GROUP // 2 + pr) * LANES
            o_ref[0, :, c0:c0 + LANES] = jnp.where(qlane < HEAD_DIM, oa, ob).astype(o_ref.dtype)


def _attn(u, bucket, rel_bias, sinks, n_heads, n_kv):
    bsz, seq, _ = u.shape
    qw = n_heads * HEAD_DIM
    kvw = 2 * n_kv * HEAD_DIM
    assert qw % kvw == 0
    kv_blk = qw // kvw
    rows = ATT_GROUP * BLOCK_Q
    return pl.pallas_call(
        _attn_body,
        grid=(bsz, seq // BLOCK_Q),
        in_specs=[
            pl.BlockSpec(bucket.shape, lambda b, i: (0, 0)),
            pl.BlockSpec(memory_space=pltpu.SMEM),
            pl.BlockSpec(sinks.shape, lambda b, i: (0, 0, 0)),
            pl.BlockSpec((1, BLOCK_Q, qw), lambda b, i: (b, i, 0)),
            pl.BlockSpec((1, BLOCK_Q, kvw), lambda b, i: (b, jnp.maximum(i - 1, 0), kv_blk)),
            pl.BlockSpec((1, BLOCK_Q, kvw), lambda b, i: (b, i, kv_blk)),
        ],
        out_specs=pl.BlockSpec((1, BLOCK_Q, qw), lambda b, i: (b, i, 0)),
        out_shape=jax.ShapeDtypeStruct((bsz, seq, qw), F32),
        scratch_shapes=[pltpu.VMEM((n_kv, rows, 2 * BLOCK_Q), F32)],
        compiler_params=_cparams("arbitrary", "arbitrary"),
        name="swa_attn",
    )(bucket, rel_bias, sinks, u, u, u)


def _split3(x):
    hi = x.astype(BF16)
    r1 = x - hi.astype(F32)
    mid = r1.astype(BF16)
    lo = (r1 - mid.astype(F32)).astype(BF16)
    return hi, mid, lo


def _mm_const_lhs(c, x):
    hi, mid, lo = _split3(x)
    return _mm(c, hi) + _mm(c, mid) + _mm(c, lo)


def _mm_const_rhs(x, c):
    hi, mid, lo = _split3(x)
    return _mm(hi, c) + _mm(mid, c) + _mm(lo, c)


def _bd(x, bd_mask):
    return jnp.where(bd_mask, jnp.concatenate([x] * DN_HEADS, axis=0), jnp.zeros((), x.dtype))


def _unbd(full, lane_head):
    out = full[:DN_CHUNK]
    for hd in range(1, DN_HEADS):
        out = jnp.where(lane_head == hd, full[hd * DN_CHUNK:(hd + 1) * DN_CHUNK], out)
    return out


def _dn_body(u_ref, cw_ref, alog_ref, dtb_ref, nrm_ref, o_ref, tail_ref, s_ref):
    hw = DN_HEADS * HEAD_DIM
    rows = o_ref.shape[1]
    c = DN_CHUNK

    @pl.when(pl.program_id(1) == 0)
    def _():
        tail_ref[...] = jnp.zeros_like(tail_ref)
        s_ref[...] = jnp.zeros_like(s_ref)

    r_i = lax.broadcasted_iota(jnp.int32, (c, hw), 0)
    l_j = lax.broadcasted_iota(jnp.int32, (c, hw), 1) % c
    lane_head = lax.broadcasted_iota(jnp.int32, (c, hw), 1) // c
    diag = r_i == l_j
    low_incl = r_i >= l_j
    low_strict = r_i > l_j
    bd_r = lax.broadcasted_iota(jnp.int32, (hw, hw), 0) // c
    bd_c = lax.broadcasted_iota(jnp.int32, (hw, hw), 1) // c
    bd_mask = bd_r == bd_c
    ones_bd = jnp.where(bd_mask, 1.0, 0.0).astype(BF16)
    t_r = lax.broadcasted_iota(jnp.int32, (rows, rows), 0)
    t_c = lax.broadcasted_iota(jnp.int32, (rows, rows), 1)
    chunk_tril = jnp.where((t_r // c == t_c // c) & (t_r >= t_c), 1.0, 0.0).astype(BF16)

    u = u_ref[0]
    qkv = _causal_conv(u[:, :3 * hw], tail_ref, cw_ref[...])
    qkv = qkv * jax.nn.sigmoid(qkv)
    q, k, v = qkv[:, :hw], qkv[:, hw:2 * hw], qkv[:, 2 * hw:]
    z = u[:, 3 * hw:4 * hw]
    beta = jax.nn.sigmoid(u[:, 4 * hw:5 * hw])
    g = -jnp.exp(alog_ref[...]) * _softplus(u[:, 5 * hw:6 * hw] + dtb_ref[...])
    q = q * lax.rsqrt(_mm_const_rhs(q * q, ones_bd) + EPS) * (HEAD_DIM ** -0.5)
    k = k * lax.rsqrt(_mm_const_rhs(k * k, ones_bd) + EPS)
    gcum = _mm_const_lhs(chunk_tril, g)

    def pmm(a, b_pk):
        return _mm(a.astype(BF16), _bd(b_pk.astype(BF16), bd_mask))

    state = s_ref[...]
    outs = []
    for ci in range(rows // c):
        rs = slice(ci * c, (ci + 1) * c)
        qc, kc, vc, bc, gc = q[rs], k[rs], v[rs], beta[rs], gcum[rs]
        g_last = gc[c - 1:c]
        eg = jnp.exp(gc)
        k_dec = kc * jnp.exp(g_last - gc)
        q_eg = qc * eg
        k_beta = kc * bc
        v_beta = vc * bc
        g_row = jnp.sum(jnp.where(diag, gc, 0.0), axis=0, keepdims=True)
        decay = jnp.exp(jnp.where(low_incl, gc - g_row, -jnp.inf))
        kq = _mm_nt(jnp.concatenate([k_beta, qc], axis=0).astype(BF16),
                    _bd(kc.astype(BF16), bd_mask))
        lmat = jnp.where(low_strict, kq[:c] * decay, 0.0)
        attn = kq[c:] * decay
        m1 = -lmat
        m2 = pmm(m1, m1)
        m34 = pmm(jnp.concatenate([m1, m2], axis=0), m2)
        psum = jnp.where(diag, 1.0, 0.0) + m1 + m2 + m34[:c]
        mp = m34[c:]
        for _ in range(3):
            both = pmm(jnp.concatenate([mp, psum], axis=0), mp)
            psum = psum + both[c:]
            mp = both[:c]
        tinv = psum + pmm(psum, mp)
        tb = tinv.astype(BF16)
        u_c = _mm(tb, _bd(v_beta.astype(BF16), bd_mask))
        w_c = _mm(tb, _bd((k_beta * eg).astype(BF16), bd_mask))
        kd_t = k_dec.astype(BF16)
        a_neg = _unbd(_mm_tn(kd_t, w_c.astype(BF16)), lane_head)
        b_pk = _unbd(_mm_tn(kd_t, u_c.astype(BF16)), lane_head)
        ab = attn.astype(BF16)
        q_eff = q_eg - _mm(ab, _bd(w_c.astype(BF16), bd_mask))
        o_loc = _mm(ab, _bd(u_c.astype(BF16), bd_mask))
        res = pmm(jnp.concatenate([a_neg, q_eff], axis=0), state)
        outs.append(res[c:] + o_loc)
        state = state * jnp.exp(g_last) - res[:c] + b_pk
    s_ref[...] = state
    o = jnp.concatenate(outs, axis=0)
    ms = _mm_const_rhs(o * o, ones_bd) * (1.0 / HEAD_DIM)
    o_ref[0] = (o * lax.rsqrt(ms + EPS) * nrm_ref[...] * (z * jax.nn.sigmoid(z))).astype(o_ref.dtype)


def _deltanet(u, conv_w, a_log, dt_bias, norm):
    bsz, seq, uw = u.shape
    hw = DN_HEADS * HEAD_DIM
    full = lambda b, t: (0, 0)
    return pl.pallas_call(
        _dn_body,
        grid=(bsz, seq // DN_TILE),
        in_specs=[
            pl.BlockSpec((1, DN_TILE, uw), lambda b, t: (b, t, 0)),
            pl.BlockSpec(conv_w.shape, full),
            pl.BlockSpec(a_log.shape, full),
            pl.BlockSpec(dt_bias.shape, full),
            pl.BlockSpec(norm.shape, full),
        ],
        out_specs=pl.BlockSpec((1, DN_TILE, hw), lambda b, t: (b, t, 0)),
        out_shape=jax.ShapeDtypeStruct((bsz, seq, hw), F32),
        scratch_shapes=[pltpu.VMEM((SUBLANES, 3 * hw), F32), pltpu.VMEM((DN_CHUNK, hw), F32)],
        compiler_params=_cparams("arbitrary", "arbitrary"),
        name="deltanet",
    )(u, conv_w, a_log, dt_bias, norm)


def _rel_bucket_table():
    qi = jnp.arange(BLOCK_Q)[:, None]
    kj = jnp.arange(2 * BLOCK_Q)[None, :]
    dist = BLOCK_Q + qi - kj
    band = (dist >= 0) & (dist < BLOCK_Q)
    d = jnp.maximum(dist, 0)
    max_exact = REL_BUCKETS // 2
    large = max_exact + (jnp.log(jnp.maximum(d, 1).astype(F32) / max_exact)
                         / math.log(REL_MAX_DIST / max_exact)
                         * (REL_BUCKETS - max_exact)).astype(jnp.int32)
    large = jnp.minimum(large, REL_BUCKETS - 1)
    return jnp.where(band, jnp.where(d < max_exact, d, large), -1).astype(jnp.int32)


def _block_diag(w):
    hh, n, _ = w.shape
    eye = jnp.eye(hh, dtype=w.dtype)
    return (eye[:, None, :, None] * w[:, :, None, :]).reshape(hh * n, hh * n)


def kernel(x, p, ffn1_norm, ffn1_w_gate, ffn1_w_up, ffn1_w_down, mix_norm, w_in, lru_conv_w, lru_conv_b, lru_w_a, lru_b_a, lru_w_x, lru_b_x, lru_lambda, attn_sinks, rel_bias, dn_conv_w, dn_a_log, dn_dt_bias, dn_norm, w_out, ffn2_norm, ffn2_w_gate, ffn2_w_up, ffn2_w_down, ple_norm, ple_w_gate, ple_w_proj, final_norm):
    bsz, seq, d = x.shape
    depth = w_in.shape[0]
    n = bsz * seq
    lru_w = lru_lambda.shape[1]
    n_heads = attn_sinks.shape[1]
    dn_w = DN_HEADS * HEAD_DIM
    n_kv = (w_in.shape[2] - 2 * lru_w - n_heads * HEAD_DIM - 4 * dn_w - 2 * DN_HEADS) // (2 * HEAD_DIM)
    att_w = (n_heads + 2 * n_kv) * HEAD_DIM
    main = 2 * lru_w + att_w + 4 * dn_w
    widths = (2 * lru_w, att_w, 6 * dn_w)
    bucket = _rel_bucket_table()
    row = lambda v: v.reshape(1, -1).astype(F32)

    h = x.reshape(n, d)
    for l in range(depth):
        h = _ffn(h, row(ffn1_norm[l]), ffn1_w_gate[l].astype(BF16), ffn1_w_up[l].astype(BF16),
                 ffn1_w_down[l].astype(BF16))
        w_aug = jnp.concatenate(
            [w_in[l, :, :main],
             jnp.repeat(w_in[l, :, main:main + DN_HEADS], HEAD_DIM, axis=1),
             jnp.repeat(w_in[l, :, main + DN_HEADS:], HEAD_DIM, axis=1)], axis=1).astype(BF16)
        u_lru, u_att, u_dn = _proj_in(h, row(mix_norm[l]), w_aug, widths)

        w_ax = jnp.concatenate([_block_diag(lru_w_a[l]), _block_diag(lru_w_x[l])], axis=1).astype(BF16)
        b_ax = jnp.concatenate([lru_b_a[l], lru_b_x[l]]).reshape(1, -1)
        y_lru = _lru(u_lru.reshape(bsz, seq, -1), lru_conv_w[l], row(lru_conv_b[l]), w_ax, b_ax,
                     row(lru_lambda[l]))

        sinks = jnp.broadcast_to(attn_sinks[l].reshape(n_kv, ATT_GROUP, 1, 1),
                                 (n_kv, ATT_GROUP, BLOCK_Q, 1)).reshape(n_kv, ATT_GROUP * BLOCK_Q, 1)
        y_att = _attn(u_att.reshape(bsz, seq, -1), bucket, rel_bias, sinks, n_heads, n_kv)

        rep = lambda v: jnp.repeat(v, HEAD_DIM).reshape(1, -1)
        y_dn = _deltanet(u_dn.reshape(bsz, seq, -1), dn_conv_w[l], rep(dn_a_log[l]), rep(dn_dt_bias[l]),
                         jnp.tile(dn_norm[l], DN_HEADS).reshape(1, -1))

        h = _proj_out(h, y_lru.reshape(n, -1), y_att.reshape(n, -1), y_dn.reshape(n, -1),
                      w_out[l].astype(BF16))
        h = _ffn(h, row(ffn2_norm[l]), ffn2_w_gate[l].astype(BF16), ffn2_w_up[l].astype(BF16),
                 ffn2_w_down[l].astype(BF16))
        h = _ple(h, p[l].reshape(n, -1), row(ple_norm[l]), ple_w_gate[l].astype(BF16),
                 ple_w_proj[l].astype(BF16), row(final_norm), final=(l == depth - 1))
    return h.reshape(bsz, seq, d)
```

```python
import functools
import math

import jax
import jax.numpy as jnp
from jax import lax
from jax.experimental import pallas as pl
from jax.experimental.pallas import tpu as pltpu

F32 = jnp.float32
BF16 = jnp.bfloat16

EPS = 1e-6
HEAD_DIM = 64
LRU_C = 8.0
CONV_W = 4
ATT_GROUP = 4
BLOCK_Q = 128
REL_BUCKETS = 32
REL_MAX_DIST = 128
LOG2E = math.log2(math.e)
DN_CHUNK = 64
DN_HEADS = 4
SUBLANES = 8
LANES = 128
MXU_DIM = 256
VMEM_LIMIT = 56 * 1024 * 1024

TOKEN_TILE = 512
FFN_CHUNK = MXU_DIM
LRU_TILE = 256
DN_TILE = 512


def _cparams(*sem):
    return pltpu.CompilerParams(dimension_semantics=sem, vmem_limit_bytes=VMEM_LIMIT)


def _rms(x, g):
    ms = jnp.mean(x * x, axis=-1, keepdims=True)
    return x * lax.rsqrt(ms + EPS) * g


def _mm(a, b):
    return jnp.dot(a, b, preferred_element_type=F32)


def _mm_nt(a, b):
    return lax.dot_general(a, b, (((1,), (1,)), ((), ())), preferred_element_type=F32)


def _mm_tn(a, b):
    return lax.dot_general(a, b, (((0,), (0,)), ((), ())), preferred_element_type=F32)


def _softplus(x):
    return jnp.maximum(x, 0.0) + jnp.log1p(jnp.exp(-jnp.abs(x)))


def _ffn_body(h_ref, g_ref, wg_ref, wu_ref, wd_ref, o_ref):
    h = h_ref[...]
    xn = _rms(h, g_ref[...]).astype(BF16)
    d_ff = wg_ref.shape[1]
    acc = jnp.zeros(h.shape, F32)
    for c in range(d_ff // FFN_CHUNK):
        cols = slice(c * FFN_CHUNK, (c + 1) * FFN_CHUNK)
        gate = _mm(xn, wg_ref[:, cols])
        up = _mm(xn, wu_ref[:, cols])
        act = (gate * jax.nn.sigmoid(gate) * up).astype(BF16)
        acc = acc + _mm(act, wd_ref[cols, :])
    o_ref[...] = h + 0.5 * acc


def _ffn(h, gain, wg, wu, wd):
    n, d = h.shape
    d_ff = wg.shape[1]
    full = lambda i: (0, 0)
    return pl.pallas_call(
        _ffn_body,
        grid=(n // TOKEN_TILE,),
        in_specs=[
            pl.BlockSpec((TOKEN_TILE, d), lambda i: (i, 0)),
            pl.BlockSpec((1, d), full),
            pl.BlockSpec((d, d_ff), full),
            pl.BlockSpec((d, d_ff), full),
            pl.BlockSpec((d_ff, d), full),
        ],
        out_specs=pl.BlockSpec((TOKEN_TILE, d), lambda i: (i, 0)),
        out_shape=jax.ShapeDtypeStruct((n, d), F32),
        compiler_params=_cparams("arbitrary"),
        name="ffn",
    )(h, gain, wg, wu, wd)


def _proj_in_body(h_ref, g_ref, w_ref, *o_refs):
    xn = _rms(h_ref[...], g_ref[...]).astype(BF16)
    off = 0
    for o_ref in o_refs:
        width = o_ref.shape[1]
        o_ref[...] = _mm(xn, w_ref[:, off:off + width]).astype(o_ref.dtype)
        off += width


def _proj_in(h, gain, w, widths):
    n, d = h.shape
    return pl.pallas_call(
        _proj_in_body,
        grid=(n // TOKEN_TILE,),
        in_specs=[
            pl.BlockSpec((TOKEN_TILE, d), lambda i: (i, 0)),
            pl.BlockSpec((1, d), lambda i: (0, 0)),
            pl.BlockSpec(w.shape, lambda i: (0, 0)),
        ],
        out_specs=[pl.BlockSpec((TOKEN_TILE, wd), lambda i: (i, 0)) for wd in widths],
        out_shape=[jax.ShapeDtypeStruct((n, wd), F32) for wd in widths],
        compiler_params=_cparams("arbitrary"),
        name="proj_in",
    )(h, gain, w)


def _proj_out_body(h_ref, ya_ref, yb_ref, yc_ref, w_ref, o_ref):
    y = jnp.concatenate([ya_ref[...], yb_ref[...], yc_ref[...]], axis=1).astype(BF16)
    o_ref[...] = h_ref[...] + _mm(y, w_ref[...])


def _proj_out(h, ya, yb, yc, w):
    n, d = h.shape
    row = lambda i: (i, 0)
    return pl.pallas_call(
        _proj_out_body,
        grid=(n // TOKEN_TILE,),
        in_specs=[
            pl.BlockSpec((TOKEN_TILE, d), row),
            pl.BlockSpec((TOKEN_TILE, ya.shape[1]), row),
            pl.BlockSpec((TOKEN_TILE, yb.shape[1]), row),
            pl.BlockSpec((TOKEN_TILE, yc.shape[1]), row),
            pl.BlockSpec(w.shape, lambda i: (0, 0)),
        ],
        out_specs=pl.BlockSpec((TOKEN_TILE, d), row),
        out_shape=jax.ShapeDtypeStruct((n, d), F32),
        compiler_params=_cparams("arbitrary"),
        name="proj_out",
    )(h, ya, yb, yc, w)


def _ple_body(h_ref, p_ref, g_ref, wg_ref, wp_ref, fg_ref, o_ref, *, final):
    h = h_ref[...]
    xn = _rms(h, g_ref[...]).astype(BF16)
    gate = jax.nn.sigmoid(_mm(xn, wg_ref[...]))
    out = h + gate * _mm(p_ref[...].astype(BF16), wp_ref[...])
    if final:
        out = _rms(out, fg_ref[...])
    o_ref[...] = out


def _ple(h, p, gain, wg, wp, final_gain, final):
    n, d = h.shape
    row = lambda i: (i, 0)
    full = lambda i: (0, 0)
    return pl.pallas_call(
        functools.partial(_ple_body, final=final),
        grid=(n // TOKEN_TILE,),
        in_specs=[
            pl.BlockSpec((TOKEN_TILE, d), row),
            pl.BlockSpec((TOKEN_TILE, p.shape[1]), row),
            pl.BlockSpec((1, d), full),
            pl.BlockSpec(wg.shape, full),
            pl.BlockSpec(wp.shape, full),
            pl.BlockSpec((1, d), full),
        ],
        out_specs=pl.BlockSpec((TOKEN_TILE, d), row),
        out_shape=jax.ShapeDtypeStruct((n, d), F32),
        compiler_params=_cparams("arbitrary"),
        name="ple",
    )(h, p, gain, wg, wp, final_gain)


def _causal_conv(x, tail_ref, w):
    rows = x.shape[0]
    ext = jnp.concatenate([tail_ref[...], x], axis=0)
    y = x * w[CONV_W - 1:CONV_W]
    for back in range(1, CONV_W):
        shifted = pltpu.roll(ext, back, 0)[SUBLANES:]
        y = y + shifted * w[CONV_W - 1 - back:CONV_W - back]
    tail_ref[...] = x[rows - SUBLANES:]
    return y


def _lru_body(u_ref, cw_ref, cb_ref, wax_ref, bax_ref, lam_ref, o_ref, tail_ref, h_ref):
    width = o_ref.shape[2]
    rows = o_ref.shape[1]

    @pl.when(pl.program_id(1) == 0)
    def _():
        tail_ref[...] = jnp.zeros_like(tail_ref)
        h_ref[...] = jnp.zeros_like(h_ref)

    u = u_ref[0]
    xr = _causal_conv(u[:, :width], tail_ref, cw_ref[...]) + cb_ref[...]
    gates = jax.nn.sigmoid(_mm(xr.astype(BF16), wax_ref[...]) + bax_ref[...])
    r, i = gates[:, :width], gates[:, width:]
    log_a = -LRU_C * r * _softplus(-lam_ref[...])
    a = jnp.exp(log_a)
    b = jnp.sqrt(-jnp.tanh(log_a) * (a * a + 1.0)) * (i * xr)

    sub = lax.broadcasted_iota(jnp.int32, a.shape, 0) % SUBLANES
    for d in (1, 2, 4):
        ok = sub >= d
        b = jnp.where(ok, a * pltpu.roll(b, d, 0) + b, b)
        a = jnp.where(ok, a * pltpu.roll(a, d, 0), a)
    carry = h_ref[...]
    outs = []
    for g in range(rows // SUBLANES):
        rs = slice(g * SUBLANES, (g + 1) * SUBLANES)
        hg = a[rs] * carry + b[rs]
        outs.append(hg)
        carry = hg[SUBLANES - 1:SUBLANES]
    h_ref[...] = carry
    hs = jnp.concatenate(outs, axis=0)
    o_ref[0] = (jax.nn.gelu(u[:, width:]) * hs).astype(o_ref.dtype)


def _lru(u, conv_w, conv_b, w_ax, b_ax, lam):
    bsz, seq, two_w = u.shape
    width = two_w // 2
    full = lambda b, t: (0, 0)
    return pl.pallas_call(
        _lru_body,
        grid=(bsz, seq // LRU_TILE),
        in_specs=[
            pl.BlockSpec((1, LRU_TILE, two_w), lambda b, t: (b, t, 0)),
            pl.BlockSpec(conv_w.shape, full),
            pl.BlockSpec(conv_b.shape, full),
            pl.BlockSpec(w_ax.shape, full),
            pl.BlockSpec(b_ax.shape, full),
            pl.BlockSpec(lam.shape, full),
        ],
        out_specs=pl.BlockSpec((1, LRU_TILE, width), lambda b, t: (b, t, 0)),
        out_shape=jax.ShapeDtypeStruct((bsz, seq, width), F32),
        scratch_shapes=[pltpu.VMEM((SUBLANES, width), F32), pltpu.VMEM((1, width), F32)],
        compiler_params=_cparams("arbitrary", "arbitrary"),
        name="rg_lru",
    )(u, conv_w, conv_b, w_ax, b_ax, lam)


def _attn_body(bucket_ref, relb_ref, sink_ref, q_ref, kvp_ref, kvc_ref, o_ref, bias_ref):
    n_kv = kvc_ref.shape[2] // (2 * HEAD_DIM)
    blk = pl.program_id(1)

    @pl.when((pl.program_id(0) == 0) & (blk == 0))
    def _():
        bucket = bucket_ref[...]
        in_prev = lax.broadcasted_iota(jnp.int32, bucket.shape, 1) < BLOCK_Q
        for head in range(n_kv * ATT_GROUP):
            tbl = jnp.full(bucket.shape, -jnp.inf, F32)
            for bk in range(REL_BUCKETS):
                tbl = jnp.where(bucket == bk, relb_ref[bk, head] * LOG2E, tbl)
            g, j = divmod(head, ATT_GROUP)
            bias_ref[1, g, j] = tbl
            bias_ref[0, g, j] = jnp.where(in_prev, -jnp.inf, tbl)

    has_prev = jnp.minimum(blk, 1)
    q = q_ref[0] * (HEAD_DIM ** -0.5 * LOG2E)
    kv = jnp.concatenate([kvp_ref[0], kvc_ref[0]], axis=0)
    kw = n_kv * HEAD_DIM
    k_all, v_all = kv[:, :kw], kv[:, kw:]
    lane = lax.broadcasted_iota(jnp.int32, (2 * BLOCK_Q, LANES), 1)
    qlane = lax.broadcasted_iota(jnp.int32, (BLOCK_Q, LANES), 1)
    ones = jnp.ones((2 * BLOCK_Q, LANES), BF16)

    for g in range(n_kv):
        tile = g // 2
        k2 = k_all[:, tile * LANES:(tile + 1) * LANES]
        v2 = v_all[:, tile * LANES:(tile + 1) * LANES]
        keep = (lane < HEAD_DIM) if g % 2 == 0 else (lane >= HEAD_DIM)
        kd = jnp.where(keep, k2, pltpu.roll(k2, HEAD_DIM, 1)).astype(BF16)
        vd = jnp.where(keep, v2, pltpu.roll(v2, HEAD_DIM, 1)).astype(BF16)
        v_aug = jnp.concatenate([vd, ones], axis=1)
        parts = []
        for pr in range(ATT_GROUP // 2):
            c0 = (g * ATT_GROUP // 2 + pr) * LANES
            qp = q[:, c0:c0 + LANES]
            parts.append(jnp.where(qlane < HEAD_DIM, qp, 0.0))
            parts.append(jnp.where(qlane >= HEAD_DIM, qp, 0.0))
        s = _mm_nt(jnp.concatenate(parts, axis=0).astype(BF16), kd)
        es, ms = [], []
        for j in range(ATT_GROUP):
            sj = s[j * BLOCK_Q:(j + 1) * BLOCK_Q] + bias_ref[has_prev, g, j]
            m = jnp.maximum(jnp.max(sj, axis=1, keepdims=True), sink_ref[g * ATT_GROUP + j] * LOG2E)
            es.append(jnp.exp2(sj - m).astype(BF16))
            ms.append(m)
        oa = _mm(jnp.concatenate(es, axis=0), v_aug)
        outs = []
        for j in range(ATT_GROUP):
            oj = oa[j * BLOCK_Q:(j + 1) * BLOCK_Q]
            den = oj[:, LANES:] + jnp.exp2(sink_ref[g * ATT_GROUP + j] * LOG2E - ms[j])
            outs.append(oj[:, :LANES] / den)
        for pr in range(ATT_GROUP // 2):
            c0 = (g * ATT_GROUP // 2 + pr) * LANES
            o_ref[0, :, c0:c0 + LANES] = jnp.where(qlane < HEAD_DIM, outs[2 * pr],
                                                   outs[2 * pr + 1]).astype(o_ref.dtype)


def _attn(u, bucket, rel_bias, sinks, n_heads, n_kv):
    bsz, seq, _ = u.shape
    qw = n_heads * HEAD_DIM
    kvw = 2 * n_kv * HEAD_DIM
    assert qw % kvw == 0
    kv_blk = qw // kvw
    return pl.pallas_call(
        _attn_body,
        grid=(bsz, seq // BLOCK_Q),
        in_specs=[
            pl.BlockSpec(bucket.shape, lambda b, i: (0, 0)),
            pl.BlockSpec(memory_space=pltpu.SMEM),
            pl.BlockSpec(memory_space=pltpu.SMEM),
            pl.BlockSpec((1, BLOCK_Q, qw), lambda b, i: (b, i, 0)),
            pl.BlockSpec((1, BLOCK_Q, kvw), lambda b, i: (b, jnp.maximum(i - 1, 0), kv_blk)),
            pl.BlockSpec((1, BLOCK_Q, kvw), lambda b, i: (b, i, kv_blk)),
        ],
        out_specs=pl.BlockSpec((1, BLOCK_Q, qw), lambda b, i: (b, i, 0)),
        out_shape=jax.ShapeDtypeStruct((bsz, seq, qw), F32),
        scratch_shapes=[pltpu.VMEM((2, n_kv, ATT_GROUP, BLOCK_Q, 2 * BLOCK_Q), F32)],
        compiler_params=_cparams("arbitrary", "arbitrary"),
        name="swa_attn",
    )(bucket, rel_bias, sinks, u, u, u)


def _split3(x):
    hi = x.astype(BF16)
    r1 = x - hi.astype(F32)
    mid = r1.astype(BF16)
    lo = (r1 - mid.astype(F32)).astype(BF16)
    return hi, mid, lo


def _mm_const_lhs(c, x):
    hi, mid, lo = _split3(x)
    return _mm(c, hi) + _mm(c, mid) + _mm(c, lo)


def _bd(x, bd_mask):
    return jnp.where(bd_mask, jnp.concatenate([x] * DN_HEADS, axis=0), jnp.zeros((), x.dtype))


def _head_transpose(x):
    xt = x.T
    return jnp.concatenate([xt[hd * DN_CHUNK:(hd + 1) * DN_CHUNK] for hd in range(DN_HEADS)], axis=1)


def _group_sum(x, ones_bd):
    hi = x.astype(BF16)
    lo = (x - hi.astype(F32)).astype(BF16)
    return _mm(hi, ones_bd) + _mm(lo, ones_bd)


def _dn_body(u_ref, cw_ref, alog_ref, dtb_ref, nrm_ref, o_ref, tail_ref, s_ref):
    hw = DN_HEADS * HEAD_DIM
    rows = o_ref.shape[1]
    c = DN_CHUNK

    @pl.when(pl.program_id(1) == 0)
    def _():
        tail_ref[...] = jnp.zeros_like(tail_ref)
        s_ref[...] = jnp.zeros_like(s_ref)

    r_i = lax.broadcasted_iota(jnp.int32, (c, hw), 0)
    l_j = lax.broadcasted_iota(jnp.int32, (c, hw), 1) % c
    diag = r_i == l_j
    low_incl = r_i >= l_j
    low_strict = r_i > l_j
    bd_r = lax.broadcasted_iota(jnp.int32, (hw, hw), 0) // c
    bd_c = lax.broadcasted_iota(jnp.int32, (hw, hw), 1) // c
    bd_mask = bd_r == bd_c
    ones_bd = jnp.where(bd_mask, 1.0, 0.0).astype(BF16)
    t_r = lax.broadcasted_iota(jnp.int32, (rows, rows), 0)
    t_c = lax.broadcasted_iota(jnp.int32, (rows, rows), 1)
    chunk_tril = jnp.where((t_r // c == t_c // c) & (t_r >= t_c), 1.0, 0.0).astype(BF16)

    u = u_ref[0]
    qkv = _causal_conv(u[:, :3 * hw], tail_ref, cw_ref[...])
    qkv = qkv * jax.nn.sigmoid(qkv)
    q, k, v = qkv[:, :hw], qkv[:, hw:2 * hw], qkv[:, 2 * hw:]
    z = u[:, 3 * hw:4 * hw]
    beta = jax.nn.sigmoid(u[:, 4 * hw:5 * hw])
    g = -jnp.exp(alog_ref[...]) * _softplus(u[:, 5 * hw:6 * hw] + dtb_ref[...])
    q = q * lax.rsqrt(_group_sum(q * q, ones_bd) + EPS) * (HEAD_DIM ** -0.5)
    k = k * lax.rsqrt(_group_sum(k * k, ones_bd) + EPS)
    gcum = _mm_const_lhs(chunk_tril, g)

    def pmm(a, b_pk):
        return _mm(a.astype(BF16), _bd(b_pk.astype(BF16), bd_mask))

    ch = range(rows // c)
    sl = [slice(i * c, (i + 1) * c) for i in ch]
    qc, kc, gc = [q[s] for s in sl], [k[s] for s in sl], [gcum[s] for s in sl]
    g_last = [gc[i][c - 1:c] for i in ch]
    eg = [jnp.exp(gc[i]) for i in ch]
    k_beta = [kc[i] * beta[sl[i]] for i in ch]
    v_beta = [v[sl[i]] * beta[sl[i]] for i in ch]
    g_row = [jnp.sum(jnp.where(diag, gc[i], 0.0), axis=0, keepdims=True) for i in ch]
    decay = [jnp.exp(jnp.where(low_incl, gc[i] - g_row[i], -jnp.inf)) for i in ch]
    kq = [_mm_nt(jnp.concatenate([k_beta[i], qc[i]], axis=0).astype(BF16), _bd(kc[i].astype(BF16), bd_mask))
          for i in ch]
    attn = [kq[i][c:] * decay[i] for i in ch]
    m1 = [-jnp.where(low_strict, kq[i][:c] * decay[i], 0.0) for i in ch]
    m2 = [pmm(m1[i], m1[i]) for i in ch]
    m34 = [pmm(jnp.concatenate([m1[i], m2[i]], axis=0), m2[i]) for i in ch]
    psum = [jnp.where(diag, 1.0, 0.0) + m1[i] + m2[i] + m34[i][:c] for i in ch]
    mp = [m34[i][c:] for i in ch]
    for _ in range(3):
        both = [pmm(jnp.concatenate([mp[i], psum[i]], axis=0), mp[i]) for i in ch]
        psum = [psum[i] + both[i][c:] for i in ch]
        mp = [both[i][:c] for i in ch]
    tb = [(psum[i] + pmm(psum[i], mp[i])).astype(BF16) for i in ch]
    u_c = [_mm(tb[i], _bd(v_beta[i].astype(BF16), bd_mask)).astype(BF16) for i in ch]
    w_c = [_mm(tb[i], _bd((k_beta[i] * eg[i]).astype(BF16), bd_mask)).astype(BF16) for i in ch]
    lhs = [jnp.concatenate([_head_transpose(kc[i] * jnp.exp(g_last[i] - gc[i])), attn[i]], axis=0).astype(BF16)
           for i in ch]
    wu = [_mm(lhs[i], jnp.concatenate([_bd(w_c[i], bd_mask), _bd(u_c[i], bd_mask)], axis=1)) for i in ch]
    q_eff = [qc[i] * eg[i] - wu[i][c:, :hw] for i in ch]
    state = s_ref[...]
    outs = []
    for i in ch:
        res = pmm(jnp.concatenate([wu[i][:c, :hw], q_eff[i]], axis=0), state)
        outs.append(res[c:] + wu[i][c:, hw:])
        state = state * jnp.exp(g_last[i]) - res[:c] + wu[i][:c, hw:]
    s_ref[...] = state
    o = jnp.concatenate(outs, axis=0)
    ms = _group_sum(o * o, ones_bd) * (1.0 / HEAD_DIM)
    o_ref[0] = (o * lax.rsqrt(ms + EPS) * nrm_ref[...] * (z * jax.nn.sigmoid(z))).astype(o_ref.dtype)


def _deltanet(u, conv_w, a_log, dt_bias, norm):
    bsz, seq, uw = u.shape
    hw = DN_HEADS * HEAD_DIM
    full = lambda b, t: (0, 0)
    return pl.pallas_call(
        _dn_body,
        grid=(bsz, seq // DN_TILE),
        in_specs=[
            pl.BlockSpec((1, DN_TILE, uw), lambda b, t: (b, t, 0)),
            pl.BlockSpec(conv_w.shape, full),
            pl.BlockSpec(a_log.shape, full),
            pl.BlockSpec(dt_bias.shape, full),
            pl.BlockSpec(norm.shape, full),
        ],
        out_specs=pl.BlockSpec((1, DN_TILE, hw), lambda b, t: (b, t, 0)),
        out_shape=jax.ShapeDtypeStruct((bsz, seq, hw), F32),
        scratch_shapes=[pltpu.VMEM((SUBLANES, 3 * hw), F32), pltpu.VMEM((DN_CHUNK, hw), F32)],
        compiler_params=_cparams("arbitrary", "arbitrary"),
        name="deltanet",
    )(u, conv_w, a_log, dt_bias, norm)


def _rel_bucket_table():
    qi = jnp.arange(BLOCK_Q)[:, None]
    kj = jnp.arange(2 * BLOCK_Q)[None, :]
    dist = BLOCK_Q + qi - kj
    band = (dist >= 0) & (dist < BLOCK_Q)
    d = jnp.maximum(dist, 0)
    max_exact = REL_BUCKETS // 2
    large = max_exact + (jnp.log(jnp.maximum(d, 1).astype(F32) / max_exact)
                         / math.log(REL_MAX_DIST / max_exact)
                         * (REL_BUCKETS - max_exact)).astype(jnp.int32)
    large = jnp.minimum(large, REL_BUCKETS - 1)
    return jnp.where(band, jnp.where(d < max_exact, d, large), -1).astype(jnp.int32)


def _block_diag(w):
    hh, n, _ = w.shape
    eye = jnp.eye(hh, dtype=w.dtype)
    return (eye[:, None, :, None] * w[:, :, None, :]).reshape(hh * n, hh * n)


def kernel(x, p, ffn1_norm, ffn1_w_gate, ffn1_w_up, ffn1_w_down, mix_norm, w_in, lru_conv_w, lru_conv_b, lru_w_a, lru_b_a, lru_w_x, lru_b_x, lru_lambda, attn_sinks, rel_bias, dn_conv_w, dn_a_log, dn_dt_bias, dn_norm, w_out, ffn2_norm, ffn2_w_gate, ffn2_w_up, ffn2_w_down, ple_norm, ple_w_gate, ple_w_proj, final_norm):
    bsz, seq, d = x.shape
    depth = w_in.shape[0]
    n = bsz * seq
    lru_w = lru_lambda.shape[1]
    n_heads = attn_sinks.shape[1]
    dn_w = DN_HEADS * HEAD_DIM
    n_kv = (w_in.shape[2] - 2 * lru_w - n_heads * HEAD_DIM - 4 * dn_w - 2 * DN_HEADS) // (2 * HEAD_DIM)
    att_w = (n_heads + 2 * n_kv) * HEAD_DIM
    main = 2 * lru_w + att_w + 4 * dn_w
    widths = (2 * lru_w, att_w, 6 * dn_w)
    bucket = _rel_bucket_table()
    row = lambda v: v.reshape(1, -1).astype(F32)

    h = x.reshape(n, d)
    for l in range(depth):
        h = _ffn(h, row(ffn1_norm[l]), ffn1_w_gate[l].astype(BF16), ffn1_w_up[l].astype(BF16),
                 ffn1_w_down[l].astype(BF16))
        w_aug = jnp.concatenate(
            [w_in[l, :, :main],
             jnp.repeat(w_in[l, :, main:main + DN_HEADS], HEAD_DIM, axis=1),
             jnp.repeat(w_in[l, :, main + DN_HEADS:], HEAD_DIM, axis=1)], axis=1).astype(BF16)
        u_lru, u_att, u_dn = _proj_in(h, row(mix_norm[l]), w_aug, widths)

        w_ax = jnp.concatenate([_block_diag(lru_w_a[l]), _block_diag(lru_w_x[l])], axis=1).astype(BF16)
        b_ax = jnp.concatenate([lru_b_a[l], lru_b_x[l]]).reshape(1, -1)
        y_lru = _lru(u_lru.reshape(bsz, seq, -1), lru_conv_w[l], row(lru_conv_b[l]), w_ax, b_ax,
                     row(lru_lambda[l]))

        y_att = _attn(u_att.reshape(bsz, seq, -1), bucket, rel_bias, attn_sinks[l], n_heads, n_kv)

        rep = lambda v: jnp.repeat(v, HEAD_DIM).reshape(1, -1)
        y_dn = _deltanet(u_dn.reshape(bsz, seq, -1), dn_conv_w[l], rep(dn_a_log[l]), rep(dn_dt_bias[l]),
                         jnp.tile(dn_norm[l], DN_HEADS).reshape(1, -1))

        h = _proj_out(h, y_lru.reshape(n, -1), y_att.reshape(n, -1), y_dn.reshape(n, -1),
                      w_out[l].astype(BF16))
        h = _ffn(h, row(ffn2_norm[l]), ffn2_w_gate[l].astype(BF16), ffn2_w_up[l].astype(BF16),
                 ffn2_w_down[l].astype(BF16))
        h = _ple(h, p[l].reshape(n, -1), row(ple_norm[l]), ple_w_gate[l].astype(BF16),
                 ple_w_proj[l].astype(BF16), row(final_norm), final=(l == depth - 1))
    return h.reshape(bsz, seq, d)
```

```python
import functools
import math

import jax
import jax.numpy as jnp
from jax import lax
from jax.experimental import pallas as pl
from jax.experimental.pallas import tpu as pltpu

F32 = jnp.float32
BF16 = jnp.bfloat16

EPS = 1e-6
HEAD_DIM = 64
LRU_C = 8.0
CONV_W = 4
ATT_GROUP = 4
BLOCK_Q = 128
REL_BUCKETS = 32
REL_MAX_DIST = 128
LOG2E = math.log2(math.e)
DN_CHUNK = 64
DN_HEADS = 4
SUBLANES = 8
LANES = 128
MXU_DIM = 256
VMEM_LIMIT = 56 * 1024 * 1024

TOKEN_TILE = 512
FFN_CHUNK = MXU_DIM
LRU_TILE = 256
DN_TILE = 512
DN_CUMSUM_ROWS = 256


def _cparams(*sem):
    return pltpu.CompilerParams(dimension_semantics=sem, vmem_limit_bytes=VMEM_LIMIT)


def _const_spec(shape):
    zeros = (0,) * len(shape)
    return pl.BlockSpec(shape, lambda *_: zeros, pipeline_mode=pl.Buffered(1))


def _rms(x, g):
    ms = jnp.mean(x * x, axis=-1, keepdims=True)
    return x * lax.rsqrt(ms + EPS) * g


def _mm(a, b):
    return jnp.dot(a, b, preferred_element_type=F32)


def _mm_nt(a, b):
    return lax.dot_general(a, b, (((1,), (1,)), ((), ())), preferred_element_type=F32)


def _swiglu_half_step(h, g_ref, wg_ref, wu_ref, wd_ref):
    xn = _rms(h, g_ref[...]).astype(BF16)
    acc = jnp.zeros(h.shape, F32)
    for c in range(wg_ref.shape[1] // FFN_CHUNK):
        cols = slice(c * FFN_CHUNK, (c + 1) * FFN_CHUNK)
        gate = _mm(xn, wg_ref[:, cols])
        up = _mm(xn, wu_ref[:, cols])
        act = (gate * jax.nn.sigmoid(gate) * up).astype(BF16)
        acc = acc + _mm(act, wd_ref[cols, :])
    return h + 0.5 * acc


def _pre_body(h_ref, g1_ref, wg_ref, wu_ref, wd_ref, gm_ref, win_ref, ho_ref, *u_refs):
    h = _swiglu_half_step(h_ref[...], g1_ref, wg_ref, wu_ref, wd_ref)
    ho_ref[...] = h
    xn = _rms(h, gm_ref[...]).astype(BF16)
    off = 0
    for u_ref in u_refs:
        width = u_ref.shape[1]
        u_ref[...] = _mm(xn, win_ref[:, off:off + width]).astype(u_ref.dtype)
        off += width


def _pre(h, g1, wg, wu, wd, gm, w_in, outs):
    n, d = h.shape
    row = lambda i: (i, 0)
    return pl.pallas_call(
        _pre_body,
        grid=(n // TOKEN_TILE,),
        in_specs=[pl.BlockSpec((TOKEN_TILE, d), row)]
        + [_const_spec(a.shape) for a in (g1, wg, wu, wd, gm, w_in)],
        out_specs=[pl.BlockSpec((TOKEN_TILE, d), row)]
        + [pl.BlockSpec((TOKEN_TILE, wd_), row) for wd_, _ in outs],
        out_shape=[jax.ShapeDtypeStruct((n, d), F32)]
        + [jax.ShapeDtypeStruct((n, wd_), dt) for wd_, dt in outs],
        compiler_params=_cparams("arbitrary"),
        name="pre",
    )(h, g1, wg, wu, wd, gm, w_in)


def _post_body(h_ref, ya_ref, yb_ref, yc_ref, p_ref, wo_ref, g2_ref, wg_ref, wu_ref, wd_ref,
               gp_ref, wpg_ref, wpp_ref, fg_ref, o_ref, *, final):
    y = jnp.concatenate([ya_ref[...], yb_ref[...], yc_ref[...]], axis=1)
    h = h_ref[...] + _mm(y, wo_ref[...])
    h = _swiglu_half_step(h, g2_ref, wg_ref, wu_ref, wd_ref)
    gate = jax.nn.sigmoid(_mm(_rms(h, gp_ref[...]).astype(BF16), wpg_ref[...]))
    h = h + gate * _mm(p_ref[...].astype(BF16), wpp_ref[...])
    if final:
        h = _rms(h, fg_ref[...])
    o_ref[...] = h


def _post(h, ya, yb, yc, p, consts, final):
    n, d = h.shape
    row = lambda i: (i, 0)
    return pl.pallas_call(
        functools.partial(_post_body, final=final),
        grid=(n // TOKEN_TILE,),
        in_specs=[pl.BlockSpec((TOKEN_TILE, a.shape[1]), row) for a in (h, ya, yb, yc, p)]
        + [_const_spec(a.shape) for a in consts],
        out_specs=pl.BlockSpec((TOKEN_TILE, d), row),
        out_shape=jax.ShapeDtypeStruct((n, d), F32),
        compiler_params=_cparams("arbitrary"),
        name="post",
    )(h, ya, yb, yc, p, *consts)


def _causal_conv(x, tail_ref, w):
    rows, ch = x.shape
    groups = rows // SUBLANES
    ext = jnp.concatenate([tail_ref[...], x], axis=0).reshape(groups + 1, SUBLANES, ch)
    sub = lax.broadcasted_iota(jnp.int32, (groups, SUBLANES, ch), 1)
    y = x * w[CONV_W - 1:CONV_W]
    for back in range(1, CONV_W):
        rot = pltpu.roll(ext, back, 1)
        shifted = jnp.where(sub >= back, rot[1:], rot[:-1]).reshape(rows, ch)
        y = y + shifted * w[CONV_W - 1 - back:CONV_W - back]
    tail_ref[...] = x[rows - SUBLANES:]
    return y


def _lru_body(u_ref, cw_ref, cb_ref, wax_ref, bax_ref, lam_ref, o_ref, tail_ref, h_ref):
    width = o_ref.shape[2]
    rows = o_ref.shape[1]
    groups = rows // SUBLANES

    @pl.when(pl.program_id(1) == 0)
    def _():
        tail_ref[...] = jnp.zeros_like(tail_ref)
        h_ref[...] = jnp.zeros_like(h_ref)

    u = u_ref[0]
    xr = _causal_conv(u[:, :width], tail_ref, cw_ref[...]) + cb_ref[...]
    gates = jax.nn.sigmoid(_mm(xr.astype(BF16), wax_ref[...]) + bax_ref[...])
    r, i = gates[:, :width], gates[:, width:]
    neg_lam = -lam_ref[...]
    softplus = jnp.maximum(neg_lam, 0.0) + jnp.log1p(jnp.exp(-jnp.abs(neg_lam)))
    log_a = -LRU_C * r * softplus
    a = jnp.exp(log_a)
    b = jnp.sqrt(-jnp.tanh(log_a) * (a * a + 1.0)) * (i * xr)

    a = a.reshape(groups, SUBLANES, width)
    b = b.reshape(groups, SUBLANES, width)
    sub = lax.broadcasted_iota(jnp.int32, a.shape, 1)
    for d in (1, 2, 4):
        ok = sub >= d
        b = jnp.where(ok, a * pltpu.roll(b, d, 1) + b, b)
        a = jnp.where(ok, a * pltpu.roll(a, d, 1), a)
    carry = h_ref[...]
    outs = []
    for g in range(groups):
        hg = a[g] * carry + b[g]
        outs.append(hg)
        carry = hg[SUBLANES - 1:SUBLANES]
    h_ref[...] = carry
    hs = jnp.concatenate(outs, axis=0)
    o_ref[0] = (jax.nn.gelu(u[:, width:]) * hs).astype(o_ref.dtype)


def _lru(u, conv_w, conv_b, w_ax, b_ax, lam):
    bsz, seq, two_w = u.shape
    width = two_w // 2
    return pl.pallas_call(
        _lru_body,
        grid=(bsz, seq // LRU_TILE),
        in_specs=[pl.BlockSpec((1, LRU_TILE, two_w), lambda b, t: (b, t, 0))]
        + [_const_spec(a.shape) for a in (conv_w, conv_b, w_ax, b_ax, lam)],
        out_specs=pl.BlockSpec((1, LRU_TILE, width), lambda b, t: (b, t, 0)),
        out_shape=jax.ShapeDtypeStruct((bsz, seq, width), BF16),
        scratch_shapes=[pltpu.VMEM((SUBLANES, width), F32), pltpu.VMEM((1, width), F32)],
        compiler_params=_cparams("arbitrary", "arbitrary"),
        name="rg_lru",
    )(u, conv_w, conv_b, w_ax, b_ax, lam)


def _attn_body(bucket_ref, relb_ref, sink_ref, q_ref, kvp_ref, kvc_ref, o_ref, bias_ref):
    n_kv = kvc_ref.shape[2] // (2 * HEAD_DIM)
    blk = pl.program_id(1)

    @pl.when((pl.program_id(0) == 0) & (blk == 0))
    def _():
        bucket = bucket_ref[...]
        in_prev = lax.broadcasted_iota(jnp.int32, bucket.shape, 1) < BLOCK_Q
        for head in range(n_kv * ATT_GROUP):
            tbl = jnp.full(bucket.shape, -jnp.inf, F32)
            for bk in range(REL_BUCKETS):
                tbl = jnp.where(bucket == bk, relb_ref[bk, head] * LOG2E, tbl)
            g, j = divmod(head, ATT_GROUP)
            bias_ref[1, g, j] = tbl
            bias_ref[0, g, j] = jnp.where(in_prev, -jnp.inf, tbl)

    has_prev = jnp.minimum(blk, 1)
    q = q_ref[0]
    kv = jnp.concatenate([kvp_ref[0], kvc_ref[0]], axis=0)
    kw = n_kv * HEAD_DIM
    k_all, v_all = kv[:, :kw], kv[:, kw:]
    lane = lax.broadcasted_iota(jnp.int32, (2 * BLOCK_Q, LANES), 1)
    qlane = lax.broadcasted_iota(jnp.int32, (BLOCK_Q, LANES), 1)
    ones = jnp.ones((2 * BLOCK_Q, LANES), BF16)
    zero = jnp.zeros((), BF16)

    for g in range(n_kv):
        tile = g // 2
        k2 = k_all[:, tile * LANES:(tile + 1) * LANES]
        v2 = v_all[:, tile * LANES:(tile + 1) * LANES]
        keep = (lane < HEAD_DIM) if g % 2 == 0 else (lane >= HEAD_DIM)
        kd = jnp.where(keep, k2, pltpu.roll(k2, HEAD_DIM, 1)).astype(BF16)
        vd = jnp.where(keep, v2, pltpu.roll(v2, HEAD_DIM, 1)).astype(BF16)
        v_aug = jnp.concatenate([vd, ones], axis=1)
        parts = []
        for pr in range(ATT_GROUP // 2):
            c0 = (g * ATT_GROUP // 2 + pr) * LANES
            qp = q[:, c0:c0 + LANES]
            parts.append(jnp.where(qlane < HEAD_DIM, qp, zero))
            parts.append(jnp.where(qlane >= HEAD_DIM, qp, zero))
        s = _mm_nt(jnp.concatenate(parts, axis=0), kd)
        es, ms = [], []
        for j in range(ATT_GROUP):
            sj = s[j * BLOCK_Q:(j + 1) * BLOCK_Q] + bias_ref[has_prev, g, j]
            m = jnp.maximum(jnp.max(sj, axis=1, keepdims=True), sink_ref[g * ATT_GROUP + j] * LOG2E)
            es.append(jnp.exp2(sj - m).astype(BF16))
            ms.append(m)
        oa = _mm(jnp.concatenate(es, axis=0), v_aug)
        outs = []
        for j in range(ATT_GROUP):
            oj = oa[j * BLOCK_Q:(j + 1) * BLOCK_Q]
            den = oj[:, LANES:] + jnp.exp2(sink_ref[g * ATT_GROUP + j] * LOG2E - ms[j])
            outs.append(oj[:, :LANES] / den)
        for pr in range(ATT_GROUP // 2):
            c0 = (g * ATT_GROUP // 2 + pr) * LANES
            o_ref[0, :, c0:c0 + LANES] = jnp.where(qlane < HEAD_DIM, outs[2 * pr],
                                                   outs[2 * pr + 1]).astype(o_ref.dtype)


def _attn(q, kv, bucket, rel_bias, sinks):
    bsz, seq, qw = q.shape
    kvw = kv.shape[2]
    n_kv = kvw // (2 * HEAD_DIM)
    return pl.pallas_call(
        _attn_body,
        grid=(bsz, seq // BLOCK_Q),
        in_specs=[
            _const_spec(bucket.shape),
            pl.BlockSpec(memory_space=pltpu.SMEM),
            pl.BlockSpec(memory_space=pltpu.SMEM),
            pl.BlockSpec((1, BLOCK_Q, qw), lambda b, i: (b, i, 0)),
            pl.BlockSpec((1, BLOCK_Q, kvw), lambda b, i: (b, jnp.maximum(i - 1, 0), 0)),
            pl.BlockSpec((1, BLOCK_Q, kvw), lambda b, i: (b, i, 0)),
        ],
        out_specs=pl.BlockSpec((1, BLOCK_Q, qw), lambda b, i: (b, i, 0)),
        out_shape=jax.ShapeDtypeStruct((bsz, seq, qw), BF16),
        scratch_shapes=[pltpu.VMEM((2, n_kv, ATT_GROUP, BLOCK_Q, 2 * BLOCK_Q), F32)],
        compiler_params=_cparams("arbitrary", "arbitrary"),
        name="swa_attn",
    )(bucket, rel_bias, sinks, q, kv, kv)


def _split3(x):
    hi = x.astype(BF16)
    r1 = x - hi.astype(F32)
    mid = r1.astype(BF16)
    lo = (r1 - mid.astype(F32)).astype(BF16)
    return hi, mid, lo


def _mm_const_lhs(c, x):
    hi, mid, lo = _split3(x)
    return _mm(c, hi) + _mm(c, mid) + _mm(c, lo)


def _bd(x, bd_mask):
    return jnp.where(bd_mask, jnp.concatenate([x] * DN_HEADS, axis=0), jnp.zeros((), x.dtype))


def _head_transpose(x):
    xt = x.T
    return jnp.concatenate([xt[hd * DN_CHUNK:(hd + 1) * DN_CHUNK] for hd in range(DN_HEADS)], axis=1)


def _dn_body(u_ref, cw_ref, alog_ref, dtb_ref, nrm_ref, o_ref, tail_ref, s_ref):
    hw = DN_HEADS * HEAD_DIM
    rows = o_ref.shape[1]
    c = DN_CHUNK

    @pl.when(pl.program_id(1) == 0)
    def _():
        tail_ref[...] = jnp.zeros_like(tail_ref)
        s_ref[...] = jnp.zeros_like(s_ref)

    r_i = lax.broadcasted_iota(jnp.int32, (c, hw), 0)
    l_j = lax.broadcasted_iota(jnp.int32, (c, hw), 1) % c
    diag = r_i == l_j
    low_incl = r_i >= l_j
    low_strict = r_i > l_j
    bd_r = lax.broadcasted_iota(jnp.int32, (hw, hw), 0) // c
    bd_c = lax.broadcasted_iota(jnp.int32, (hw, hw), 1) // c
    bd_mask = bd_r == bd_c
    ones_bd = jnp.where(bd_mask, 1.0, 0.0).astype(BF16)
    t_r = lax.broadcasted_iota(jnp.int32, (DN_CUMSUM_ROWS, DN_CUMSUM_ROWS), 0)
    t_c = lax.broadcasted_iota(jnp.int32, (DN_CUMSUM_ROWS, DN_CUMSUM_ROWS), 1)
    chunk_tril = jnp.where((t_r // c == t_c // c) & (t_r >= t_c), 1.0, 0.0).astype(BF16)

    def group_sum(x):
        return _mm(x.astype(BF16), ones_bd)

    u = u_ref[0]
    qkv = _causal_conv(u[:, :3 * hw], tail_ref, cw_ref[...])
    qkv = qkv * jax.nn.sigmoid(qkv)
    q, k, v = qkv[:, :hw], qkv[:, hw:2 * hw], qkv[:, 2 * hw:]
    z = u[:, 3 * hw:4 * hw]
    beta = jax.nn.sigmoid(u[:, 4 * hw:5 * hw])
    a = u[:, 5 * hw:6 * hw] + dtb_ref[...]
    g = -jnp.exp(alog_ref[...]) * (jnp.maximum(a, 0.0) + jnp.log(1.0 + jnp.exp(-jnp.abs(a))))
    q = q * (lax.rsqrt(group_sum(q * q) + EPS) * (HEAD_DIM ** -0.5))
    k = k * lax.rsqrt(group_sum(k * k) + EPS)
    gcum = jnp.concatenate(
        [_mm_const_lhs(chunk_tril, g[r0:r0 + DN_CUMSUM_ROWS]) for r0 in range(0, rows, DN_CUMSUM_ROWS)], axis=0)

    def pmm(a_pk, b_pk):
        return _mm(a_pk.astype(BF16), _bd(b_pk.astype(BF16), bd_mask))

    ch = range(rows // c)
    sl = [slice(i * c, (i + 1) * c) for i in ch]
    qc, kc, gc = [q[s] for s in sl], [k[s] for s in sl], [gcum[s] for s in sl]
    g_last = [gc[i][c - 1:c] for i in ch]
    eg = [jnp.exp(gc[i]) for i in ch]
    k_beta = [kc[i] * beta[sl[i]] for i in ch]
    v_beta = [v[sl[i]] * beta[sl[i]] for i in ch]
    g_row = [jnp.sum(jnp.where(diag, gc[i], 0.0), axis=0, keepdims=True) for i in ch]
    decay = [jnp.exp(jnp.where(low_incl, gc[i] - g_row[i], -jnp.inf)) for i in ch]
    kq = [_mm_nt(jnp.concatenate([k_beta[i], qc[i]], axis=0).astype(BF16), _bd(kc[i].astype(BF16), bd_mask))
          for i in ch]
    attn = [kq[i][c:] * decay[i] for i in ch]
    m1 = [-jnp.where(low_strict, kq[i][:c] * decay[i], 0.0) for i in ch]
    m2 = [pmm(m1[i], m1[i]) for i in ch]
    m34 = [pmm(jnp.concatenate([m1[i], m2[i]], axis=0), m2[i]) for i in ch]
    psum = [jnp.where(diag, 1.0, 0.0) + m1[i] + m2[i] + m34[i][:c] for i in ch]
    mp = [m34[i][c:] for i in ch]
    for _ in range(3):
        both = [pmm(jnp.concatenate([mp[i], psum[i]], axis=0), mp[i]) for i in ch]
        psum = [psum[i] + both[i][c:] for i in ch]
        mp = [both[i][:c] for i in ch]
    tb = [(psum[i] + pmm(psum[i], mp[i])).astype(BF16) for i in ch]
    u_c = [_mm(tb[i], _bd(v_beta[i].astype(BF16), bd_mask)).astype(BF16) for i in ch]
    w_c = [_mm(tb[i], _bd((k_beta[i] * eg[i]).astype(BF16), bd_mask)).astype(BF16) for i in ch]
    lhs = [jnp.concatenate([_head_transpose(kc[i] * jnp.exp(g_last[i] - gc[i])), attn[i]], axis=0).astype(BF16)
           for i in ch]
    wu = [_mm(lhs[i], jnp.concatenate([_bd(w_c[i], bd_mask), _bd(u_c[i], bd_mask)], axis=1)) for i in ch]
    q_eff = [qc[i] * eg[i] - wu[i][c:, :hw] for i in ch]
    state = s_ref[...]
    outs = []
    for i in ch:
        res = pmm(jnp.concatenate([wu[i][:c, :hw], q_eff[i]], axis=0), state)
        outs.append(res[c:] + wu[i][c:, hw:])
        state = state * jnp.exp(g_last[i]) - res[:c] + wu[i][:c, hw:]
    s_ref[...] = state
    o = jnp.concatenate(outs, axis=0)
    ms = group_sum(o * o) * (1.0 / HEAD_DIM)
    o_ref[0] = (o * lax.rsqrt(ms + EPS) * nrm_ref[...] * (z * jax.nn.sigmoid(z))).astype(o_ref.dtype)


def _deltanet(u, conv_w, a_log, dt_bias, norm):
    bsz, seq, uw = u.shape
    hw = DN_HEADS * HEAD_DIM
    return pl.pallas_call(
        _dn_body,
        grid=(bsz, seq // DN_TILE),
        in_specs=[pl.BlockSpec((1, DN_TILE, uw), lambda b, t: (b, t, 0))]
        + [_const_spec(a.shape) for a in (conv_w, a_log, dt_bias, norm)],
        out_specs=pl.BlockSpec((1, DN_TILE, hw), lambda b, t: (b, t, 0)),
        out_shape=jax.ShapeDtypeStruct((bsz, seq, hw), BF16),
        scratch_shapes=[pltpu.VMEM((SUBLANES, 3 * hw), F32), pltpu.VMEM((DN_CHUNK, hw), F32)],
        compiler_params=_cparams("arbitrary", "arbitrary"),
        name="deltanet",
    )(u, conv_w, a_log, dt_bias, norm)


def _rel_bucket_table():
    qi = jnp.arange(BLOCK_Q)[:, None]
    kj = jnp.arange(2 * BLOCK_Q)[None, :]
    dist = BLOCK_Q + qi - kj
    band = (dist >= 0) & (dist < BLOCK_Q)
    d = jnp.maximum(dist, 0)
    max_exact = REL_BUCKETS // 2
    large = max_exact + (jnp.log(jnp.maximum(d, 1).astype(F32) / max_exact)
                         / math.log(REL_MAX_DIST / max_exact)
                         * (REL_BUCKETS - max_exact)).astype(jnp.int32)
    large = jnp.minimum(large, REL_BUCKETS - 1)
    return jnp.where(band, jnp.where(d < max_exact, d, large), -1).astype(jnp.int32)


def _block_diag(w):
    hh, n, _ = w.shape
    eye = jnp.eye(hh, dtype=w.dtype)
    return (eye[:, None, :, None] * w[:, :, None, :]).reshape(hh * n, hh * n)


def kernel(x, p, ffn1_norm, ffn1_w_gate, ffn1_w_up, ffn1_w_down, mix_norm, w_in, lru_conv_w, lru_conv_b, lru_w_a, lru_b_a, lru_w_x, lru_b_x, lru_lambda, attn_sinks, rel_bias, dn_conv_w, dn_a_log, dn_dt_bias, dn_norm, w_out, ffn2_norm, ffn2_w_gate, ffn2_w_up, ffn2_w_down, ple_norm, ple_w_gate, ple_w_proj, final_norm):
    bsz, seq, d = x.shape
    depth = w_in.shape[0]
    n = bsz * seq
    lru_w = lru_lambda.shape[1]
    q_w = attn_sinks.shape[1] * HEAD_DIM
    dn_w = DN_HEADS * HEAD_DIM
    kv_w = w_in.shape[2] - 2 * lru_w - q_w - 4 * dn_w - 2 * DN_HEADS
    q0 = 2 * lru_w
    main = q0 + q_w + kv_w + 4 * dn_w
    groups = ((2 * lru_w, F32), (q_w, BF16), (kv_w, F32), (6 * dn_w, F32))
    bucket = _rel_bucket_table()
    row = lambda v: v.reshape(1, -1).astype(F32)
    rep = lambda v: jnp.repeat(v, HEAD_DIM).reshape(1, -1)
    bf = lambda w: w.astype(BF16)

    h = x.reshape(n, d)
    for l in range(depth):
        w_aug = bf(jnp.concatenate(
            [w_in[l, :, :q0],
             w_in[l, :, q0:q0 + q_w] * (HEAD_DIM ** -0.5 * LOG2E),
             w_in[l, :, q0 + q_w:main],
             jnp.repeat(w_in[l, :, main:main + DN_HEADS], HEAD_DIM, axis=1),
             jnp.repeat(w_in[l, :, main + DN_HEADS:], HEAD_DIM, axis=1)], axis=1))
        h, u_lru, u_q, u_kv, u_dn = _pre(
            h, row(ffn1_norm[l]), bf(ffn1_w_gate[l]), bf(ffn1_w_up[l]), bf(ffn1_w_down[l]),
            row(mix_norm[l]), w_aug, groups)

        w_ax = bf(jnp.concatenate([_block_diag(lru_w_a[l]), _block_diag(lru_w_x[l])], axis=1))
        b_ax = jnp.concatenate([lru_b_a[l], lru_b_x[l]]).reshape(1, -1)
        y_lru = _lru(u_lru.reshape(bsz, seq, -1), lru_conv_w[l], row(lru_conv_b[l]), w_ax, b_ax,
                     row(lru_lambda[l]))
        y_att = _attn(u_q.reshape(bsz, seq, -1), u_kv.reshape(bsz, seq, -1), bucket, rel_bias, attn_sinks[l])
        y_dn = _deltanet(u_dn.reshape(bsz, seq, -1), dn_conv_w[l], rep(dn_a_log[l]), rep(dn_dt_bias[l]),
                         jnp.tile(dn_norm[l], DN_HEADS).reshape(1, -1))

        consts = (bf(w_out[l]), row(ffn2_norm[l]), bf(ffn2_w_gate[l]), bf(ffn2_w_up[l]), bf(ffn2_w_down[l]),
                  row(ple_norm[l]), bf(ple_w_gate[l]), bf(ple_w_proj[l]), row(final_norm))
        h = _post(h, y_lru.reshape(n, -1), y_att.reshape(n, -1), y_dn.reshape(n, -1), p[l].reshape(n, -1),
                  consts, final=(l == depth - 1))
    return h.reshape(bsz, seq, d)
```

```python
import functools
import math

import jax
import jax.numpy as jnp
from jax import lax
from jax.experimental import pallas as pl
from jax.experimental.pallas import tpu as pltpu

F32 = jnp.float32
BF16 = jnp.bfloat16

EPS = 1e-6
HEAD_DIM = 64
LRU_C = 8.0
CONV_W = 4
ATT_GROUP = 4
BLOCK_Q = 128
REL_BUCKETS = 32
REL_MAX_DIST = 128
LOG2E = math.log2(math.e)
DN_CHUNK = 64
DN_HEADS = 4
SUBLANES = 8
LANES = 128
MXU_DIM = 256
VMEM_LIMIT = 56 * 1024 * 1024

TOKEN_TILE = 512
FFN_CHUNK = MXU_DIM
LRU_TILE = 512
ATT_TILE = 512
DN_TILE = 512
DN_CUMSUM_ROWS = 256


def _cparams(*sem):
    return pltpu.CompilerParams(dimension_semantics=sem, vmem_limit_bytes=VMEM_LIMIT)


def _const_spec(shape):
    zeros = (0,) * len(shape)
    return pl.BlockSpec(shape, lambda *_: zeros, pipeline_mode=pl.Buffered(1))


def _layer_spec(shape, layer):
    idx = (layer,) + (0,) * (len(shape) - 1)
    return pl.BlockSpec((None,) + tuple(shape[1:]), lambda *_: idx, pipeline_mode=pl.Buffered(1))


def _rms(x, g):
    ms = jnp.mean(x * x, axis=-1, keepdims=True)
    return x * lax.rsqrt(ms + EPS) * g


def _mm(a, b):
    return jnp.dot(a, b, preferred_element_type=F32)


def _mm_nt(a, b):
    return lax.dot_general(a, b, (((1,), (1,)), ((), ())), preferred_element_type=F32)


def _swiglu_half_step(h, g_ref, wg_ref, wu_ref, wd_ref):
    xn = _rms(h, g_ref[...]).astype(BF16)
    acc = jnp.zeros(h.shape, F32)
    for c in range(wg_ref.shape[1] // FFN_CHUNK):
        cols = slice(c * FFN_CHUNK, (c + 1) * FFN_CHUNK)
        gate = _mm(xn, wg_ref[:, cols])
        up = _mm(xn, wu_ref[:, cols])
        act = (gate * jax.nn.sigmoid(gate) * up).astype(BF16)
        acc = acc + _mm(act, wd_ref[cols, :])
    return h + 0.5 * acc


def _pre_body(h_ref, g1_ref, wg_ref, wu_ref, wd_ref, gm_ref, win_ref, ho_ref, *u_refs):
    h = _swiglu_half_step(h_ref[...], g1_ref, wg_ref, wu_ref, wd_ref)
    ho_ref[...] = h
    xn = _rms(h, gm_ref[...]).astype(BF16)
    off = 0
    for u_ref in u_refs:
        width = u_ref.shape[1]
        u_ref[...] = _mm(xn, win_ref[:, off:off + width]).astype(u_ref.dtype)
        off += width


def _pre(h, layer, g1, wg, wu, wd, gm, w_in, outs):
    n, d = h.shape
    row = lambda i: (i, 0)
    return pl.pallas_call(
        _pre_body,
        grid=(n // TOKEN_TILE,),
        in_specs=[pl.BlockSpec((TOKEN_TILE, d), row), _const_spec(g1.shape)]
        + [_layer_spec(a.shape, layer) for a in (wg, wu, wd)]
        + [_const_spec(gm.shape), _const_spec(w_in.shape)],
        out_specs=[pl.BlockSpec((TOKEN_TILE, d), row)]
        + [pl.BlockSpec((TOKEN_TILE, wd_), row) for wd_, _ in outs],
        out_shape=[jax.ShapeDtypeStruct((n, d), F32)]
        + [jax.ShapeDtypeStruct((n, wd_), dt) for wd_, dt in outs],
        compiler_params=_cparams("arbitrary"),
        name="pre",
    )(h, g1, wg, wu, wd, gm, w_in)


def _post_body(h_ref, ya_ref, yb_ref, yc_ref, p_ref, wo_ref, g2_ref, wg_ref, wu_ref, wd_ref,
               gp_ref, wpg_ref, wpp_ref, fg_ref, o_ref, *, final):
    y = jnp.concatenate([ya_ref[...], yb_ref[...], yc_ref[...]], axis=1)
    h = h_ref[...] + _mm(y, wo_ref[...])
    h = _swiglu_half_step(h, g2_ref, wg_ref, wu_ref, wd_ref)
    gate = jax.nn.sigmoid(_mm(_rms(h, gp_ref[...]).astype(BF16), wpg_ref[...]))
    h = h + gate * _mm(p_ref[...].astype(BF16), wpp_ref[...])
    if final:
        h = _rms(h, fg_ref[...])
    o_ref[...] = h


def _post(h, ya, yb, yc, p, layer, wo, g2, wg, wu, wd, gp, wpg, wpp, fg, final):
    n, d = h.shape
    row = lambda i: (i, 0)
    stack = lambda a: _layer_spec(a.shape, layer)
    gain = lambda a: _const_spec(a.shape)
    return pl.pallas_call(
        functools.partial(_post_body, final=final),
        grid=(n // TOKEN_TILE,),
        in_specs=[pl.BlockSpec((TOKEN_TILE, a.shape[1]), row) for a in (h, ya, yb, yc)]
        + [pl.BlockSpec((None, TOKEN_TILE, p.shape[2]), lambda i: (layer, i, 0))]
        + [stack(wo), gain(g2), stack(wg), stack(wu), stack(wd), gain(gp), stack(wpg), stack(wpp), gain(fg)],
        out_specs=pl.BlockSpec((TOKEN_TILE, d), row),
        out_shape=jax.ShapeDtypeStruct((n, d), F32),
        compiler_params=_cparams("arbitrary"),
        name="post",
    )(h, ya, yb, yc, p, wo, g2, wg, wu, wd, gp, wpg, wpp, fg)


def _causal_conv(x, tail_ref, w):
    rows, ch = x.shape
    groups = rows // SUBLANES
    ext = jnp.concatenate([tail_ref[...], x], axis=0).reshape(groups + 1, SUBLANES, ch)
    sub = lax.broadcasted_iota(jnp.int32, (groups, SUBLANES, ch), 1)
    y = x * w[CONV_W - 1:CONV_W]
    for back in range(1, CONV_W):
        rot = pltpu.roll(ext, back, 1)
        shifted = jnp.where(sub >= back, rot[1:], rot[:-1]).reshape(rows, ch)
        y = y + shifted * w[CONV_W - 1 - back:CONV_W - back]
    tail_ref[...] = x[rows - SUBLANES:]
    return y


def _lru_body(u_ref, cw_ref, cb_ref, wax_ref, bax_ref, lam_ref, o_ref, tail_ref, h_ref):
    width = o_ref.shape[2]
    rows = o_ref.shape[1]
    groups = rows // SUBLANES

    @pl.when(pl.program_id(1) == 0)
    def _():
        tail_ref[...] = jnp.zeros_like(tail_ref)
        h_ref[...] = jnp.zeros_like(h_ref)

    u = u_ref[0]
    xr = _causal_conv(u[:, :width], tail_ref, cw_ref[...]) + cb_ref[...]
    gates = jax.nn.sigmoid(_mm(xr.astype(BF16), wax_ref[...]) + bax_ref[...])
    r, i = gates[:, :width], gates[:, width:]
    neg_lam = -lam_ref[...]
    softplus = jnp.maximum(neg_lam, 0.0) + jnp.log1p(jnp.exp(-jnp.abs(neg_lam)))
    log_a = -LRU_C * r * softplus
    a = jnp.exp(log_a)
    b = jnp.sqrt(-jnp.tanh(log_a) * (a * a + 1.0)) * (i * xr)

    a = a.reshape(groups, SUBLANES, width)
    b = b.reshape(groups, SUBLANES, width)
    sub = lax.broadcasted_iota(jnp.int32, a.shape, 1)
    for d in (1, 2, 4):
        ok = sub >= d
        b = jnp.where(ok, a * pltpu.roll(b, d, 1) + b, b)
        a = jnp.where(ok, a * pltpu.roll(a, d, 1), a)
    carry = h_ref[...]
    outs = []
    for g in range(groups):
        hg = a[g] * carry + b[g]
        outs.append(hg)
        carry = hg[SUBLANES - 1:SUBLANES]
    h_ref[...] = carry
    hs = jnp.concatenate(outs, axis=0)
    o_ref[0] = (jax.nn.gelu(u[:, width:]) * hs).astype(o_ref.dtype)


def _lru(u, conv_w, conv_b, w_ax, b_ax, lam):
    bsz, seq, two_w = u.shape
    width = two_w // 2
    return pl.pallas_call(
        _lru_body,
        grid=(bsz, seq // LRU_TILE),
        in_specs=[pl.BlockSpec((1, LRU_TILE, two_w), lambda b, t: (b, t, 0))]
        + [_const_spec(a.shape) for a in (conv_w, conv_b, w_ax, b_ax, lam)],
        out_specs=pl.BlockSpec((1, LRU_TILE, width), lambda b, t: (b, t, 0)),
        out_shape=jax.ShapeDtypeStruct((bsz, seq, width), BF16),
        scratch_shapes=[pltpu.VMEM((SUBLANES, width), F32), pltpu.VMEM((1, width), F32)],
        compiler_params=_cparams("arbitrary", "arbitrary"),
        name="rg_lru",
    )(u, conv_w, conv_b, w_ax, b_ax, lam)


def _attn_body(bucket_ref, relb_ref, sink_ref, q_ref, kvp_ref, kvc_ref, o_ref, bias_ref):
    n_kv = kvc_ref.shape[2] // (2 * HEAD_DIM)
    step = pl.program_id(1)

    @pl.when((pl.program_id(0) == 0) & (step == 0))
    def _():
        bucket = bucket_ref[...]
        in_prev = lax.broadcasted_iota(jnp.int32, bucket.shape, 1) < BLOCK_Q
        for head in range(n_kv * ATT_GROUP):
            tbl = jnp.full(bucket.shape, -jnp.inf, F32)
            for bk in range(REL_BUCKETS):
                tbl = jnp.where(bucket == bk, relb_ref[bk, head] * LOG2E, tbl)
            g, j = divmod(head, ATT_GROUP)
            bias_ref[1, g, j] = tbl
            bias_ref[0, g, j] = jnp.where(in_prev, -jnp.inf, tbl)

    n_blk = q_ref.shape[1] // BLOCK_Q
    kw = n_kv * HEAD_DIM
    lane = lax.broadcasted_iota(jnp.int32, (BLOCK_Q, LANES), 1)
    ones = jnp.ones((2 * BLOCK_Q, LANES), BF16)
    zero = jnp.zeros((), BF16)
    sinks = [sink_ref[hd] * LOG2E for hd in range(n_kv * ATT_GROUP)]

    kv_blocks = [kvp_ref[0]] + [kvc_ref[0, j * BLOCK_Q:(j + 1) * BLOCK_Q] for j in range(n_blk)]
    kd, vd = [], []
    for kvb in kv_blocks:
        kd_g, vd_g = [], []
        for g in range(n_kv):
            tile = g // 2
            k2 = kvb[:, tile * LANES:(tile + 1) * LANES]
            v2 = kvb[:, kw + tile * LANES:kw + (tile + 1) * LANES]
            keep = (lane < HEAD_DIM) if g % 2 == 0 else (lane >= HEAD_DIM)
            kd_g.append(jnp.where(keep, k2, pltpu.roll(k2, HEAD_DIM, 1)).astype(BF16))
            vd_g.append(jnp.where(keep, v2, pltpu.roll(v2, HEAD_DIM, 1)).astype(BF16))
        kd.append(kd_g)
        vd.append(vd_g)

    probs = [(sb, g) for sb in range(n_blk) for g in range(n_kv)]
    scores = []
    for sb, g in probs:
        parts = []
        for pr in range(ATT_GROUP // 2):
            c0 = (g * ATT_GROUP // 2 + pr) * LANES
            qp = q_ref[0, sb * BLOCK_Q:(sb + 1) * BLOCK_Q, c0:c0 + LANES]
            parts.append(jnp.where(lane < HEAD_DIM, qp, zero))
            parts.append(jnp.where(lane >= HEAD_DIM, qp, zero))
        keys = jnp.concatenate([kd[sb][g], kd[sb + 1][g]], axis=0)
        scores.append(_mm_nt(jnp.concatenate(parts, axis=0), keys))
    es, ms = [], []
    for (sb, g), s in zip(probs, scores):
        has_prev = jnp.minimum(step, 1) if sb == 0 else 1
        e_g, m_g = [], []
        for j in range(ATT_GROUP):
            sj = s[j * BLOCK_Q:(j + 1) * BLOCK_Q] + bias_ref[has_prev, g, j]
            m = jnp.maximum(jnp.max(sj, axis=1, keepdims=True), sinks[g * ATT_GROUP + j])
            e_g.append(jnp.exp2(sj - m).astype(BF16))
            m_g.append(m)
        es.append(jnp.concatenate(e_g, axis=0))
        ms.append(m_g)
    oas = []
    for (sb, g), e in zip(probs, es):
        v_aug = jnp.concatenate([jnp.concatenate([vd[sb][g], vd[sb + 1][g]], axis=0), ones], axis=1)
        oas.append(_mm(e, v_aug))
    for (sb, g), oa, m_g in zip(probs, oas, ms):
        outs = []
        for j in range(ATT_GROUP):
            oj = oa[j * BLOCK_Q:(j + 1) * BLOCK_Q]
            den = oj[:, LANES:] + jnp.exp2(sinks[g * ATT_GROUP + j] - m_g[j])
            outs.append(oj[:, :LANES] / den)
        for pr in range(ATT_GROUP // 2):
            c0 = (g * ATT_GROUP // 2 + pr) * LANES
            o_ref[0, sb * BLOCK_Q:(sb + 1) * BLOCK_Q, c0:c0 + LANES] = jnp.where(
                lane < HEAD_DIM, outs[2 * pr], outs[2 * pr + 1]).astype(o_ref.dtype)


def _attn(q, kv, bucket, rel_bias, sinks):
    bsz, seq, qw = q.shape
    kvw = kv.shape[2]
    n_kv = kvw // (2 * HEAD_DIM)
    n_blk = ATT_TILE // BLOCK_Q
    return pl.pallas_call(
        _attn_body,
        grid=(bsz, seq // ATT_TILE),
        in_specs=[
            _const_spec(bucket.shape),
            pl.BlockSpec(memory_space=pltpu.SMEM),
            pl.BlockSpec(memory_space=pltpu.SMEM),
            pl.BlockSpec((1, ATT_TILE, qw), lambda b, i: (b, i, 0)),
            pl.BlockSpec((1, BLOCK_Q, kvw), lambda b, i: (b, jnp.maximum(i * n_blk - 1, 0), 0)),
            pl.BlockSpec((1, ATT_TILE, kvw), lambda b, i: (b, i, 0)),
        ],
        out_specs=pl.BlockSpec((1, ATT_TILE, qw), lambda b, i: (b, i, 0)),
        out_shape=jax.ShapeDtypeStruct((bsz, seq, qw), BF16),
        scratch_shapes=[pltpu.VMEM((2, n_kv, ATT_GROUP, BLOCK_Q, 2 * BLOCK_Q), F32)],
        compiler_params=_cparams("arbitrary", "arbitrary"),
        name="swa_attn",
    )(bucket, rel_bias, sinks, q, kv, kv)


def _split3(x):
    hi = x.astype(BF16)
    r1 = x - hi.astype(F32)
    mid = r1.astype(BF16)
    lo = (r1 - mid.astype(F32)).astype(BF16)
    return hi, mid, lo


def _mm_const_lhs(c, x):
    hi, mid, lo = _split3(x)
    return _mm(c, hi) + _mm(c, mid) + _mm(c, lo)


def _bd(x, bd_mask):
    return jnp.where(bd_mask, jnp.concatenate([x] * DN_HEADS, axis=0), jnp.zeros((), x.dtype))


def _head_transpose(x):
    xt = x.T
    return jnp.concatenate([xt[hd * DN_CHUNK:(hd + 1) * DN_CHUNK] for hd in range(DN_HEADS)], axis=1)


def _dn_body(u_ref, cw_ref, alog_ref, dtb_ref, nrm_ref, o_ref, tail_ref, s_ref):
    hw = DN_HEADS * HEAD_DIM
    rows = o_ref.shape[1]
    c = DN_CHUNK

    @pl.when(pl.program_id(1) == 0)
    def _():
        tail_ref[...] = jnp.zeros_like(tail_ref)
        s_ref[...] = jnp.zeros_like(s_ref)

    r_i = lax.broadcasted_iota(jnp.int32, (c, hw), 0)
    l_j = lax.broadcasted_iota(jnp.int32, (c, hw), 1) % c
    diag = r_i == l_j
    low_incl = r_i >= l_j
    low_strict = r_i > l_j
    bd_r = lax.broadcasted_iota(jnp.int32, (hw, hw), 0) // c
    bd_c = lax.broadcasted_iota(jnp.int32, (hw, hw), 1) // c
    bd_mask = bd_r == bd_c
    ones_bd = jnp.where(bd_mask, 1.0, 0.0).astype(BF16)
    t_r = lax.broadcasted_iota(jnp.int32, (DN_CUMSUM_ROWS, DN_CUMSUM_ROWS), 0)
    t_c = lax.broadcasted_iota(jnp.int32, (DN_CUMSUM_ROWS, DN_CUMSUM_ROWS), 1)
    chunk_tril = jnp.where((t_r // c == t_c // c) & (t_r >= t_c), 1.0, 0.0).astype(BF16)

    def group_sum(x):
        return _mm(x.astype(BF16), ones_bd)

    u = u_ref[0]
    qkv = _causal_conv(u[:, :3 * hw], tail_ref, cw_ref[...])
    qkv = qkv * jax.nn.sigmoid(qkv)
    q, k, v = qkv[:, :hw], qkv[:, hw:2 * hw], qkv[:, 2 * hw:]
    z = u[:, 3 * hw:4 * hw]
    beta = jax.nn.sigmoid(u[:, 4 * hw:5 * hw])
    a = u[:, 5 * hw:6 * hw] + dtb_ref[...]
    g = -jnp.exp(alog_ref[...]) * (jnp.maximum(a, 0.0) + jnp.log(1.0 + jnp.exp(-jnp.abs(a))))
    q = q * (lax.rsqrt(group_sum(q * q) + EPS) * (HEAD_DIM ** -0.5))
    k = k * lax.rsqrt(group_sum(k * k) + EPS)
    gcum = jnp.concatenate(
        [_mm_const_lhs(chunk_tril, g[r0:r0 + DN_CUMSUM_ROWS]) for r0 in range(0, rows, DN_CUMSUM_ROWS)], axis=0)

    def pmm(a_pk, b_pk):
        return _mm(a_pk.astype(BF16), _bd(b_pk.astype(BF16), bd_mask))

    ch = range(rows // c)
    sl = [slice(i * c, (i + 1) * c) for i in ch]
    qc, kc, gc = [q[s] for s in sl], [k[s] for s in sl], [gcum[s] for s in sl]
    g_last = [gc[i][c - 1:c] for i in ch]
    eg = [jnp.exp(gc[i]) for i in ch]
    k_beta = [kc[i] * beta[sl[i]] for i in ch]
    v_beta = [v[sl[i]] * beta[sl[i]] for i in ch]
    g_row = [jnp.sum(jnp.where(diag, gc[i], 0.0), axis=0, keepdims=True) for i in ch]
    decay = [jnp.exp(jnp.where(low_incl, gc[i] - g_row[i], -jnp.inf)) for i in ch]
    kq = [_mm_nt(jnp.concatenate([k_beta[i], qc[i]], axis=0).astype(BF16), _bd(kc[i].astype(BF16), bd_mask))
          for i in ch]
    attn = [kq[i][c:] * decay[i] for i in ch]
    m1 = [-jnp.where(low_strict, kq[i][:c] * decay[i], 0.0) for i in ch]
    m2 = [pmm(m1[i], m1[i]) for i in ch]
    m34 = [pmm(jnp.concatenate([m1[i], m2[i]], axis=0), m2[i]) for i in ch]
    psum = [jnp.where(diag, 1.0, 0.0) + m1[i] + m2[i] + m34[i][:c] for i in ch]
    mp = [m34[i][c:] for i in ch]
    for _ in range(3):
        both = [pmm(jnp.concatenate([mp[i], psum[i]], axis=0), mp[i]) for i in ch]
        psum = [psum[i] + both[i][c:] for i in ch]
        mp = [both[i][:c] for i in ch]
    tb = [(psum[i] + pmm(psum[i], mp[i])).astype(BF16) for i in ch]
    u_c = [_mm(tb[i], _bd(v_beta[i].astype(BF16), bd_mask)).astype(BF16) for i in ch]
    w_c = [_mm(tb[i], _bd((k_beta[i] * eg[i]).astype(BF16), bd_mask)).astype(BF16) for i in ch]
    lhs = [jnp.concatenate([_head_transpose(kc[i] * jnp.exp(g_last[i] - gc[i])), attn[i]], axis=0).astype(BF16)
           for i in ch]
    wu = [_mm(lhs[i], jnp.concatenate([_bd(w_c[i], bd_mask), _bd(u_c[i], bd_mask)], axis=1)) for i in ch]
    q_eff = [qc[i] * eg[i] - wu[i][c:, :hw] for i in ch]
    state = s_ref[...]
    outs = []
    for i in ch:
        res = pmm(jnp.concatenate([wu[i][:c, :hw], q_eff[i]], axis=0), state)
        outs.append(res[c:] + wu[i][c:, hw:])
        state = state * jnp.exp(g_last[i]) - res[:c] + wu[i][:c, hw:]
    s_ref[...] = state
    o = jnp.concatenate(outs, axis=0)
    ms = group_sum(o * o) * (1.0 / HEAD_DIM)
    o_ref[0] = (o * lax.rsqrt(ms + EPS) * nrm_ref[...] * (z * jax.nn.sigmoid(z))).astype(o_ref.dtype)


def _deltanet(u, conv_w, a_log, dt_bias, norm):
    bsz, seq, uw = u.shape
    hw = DN_HEADS * HEAD_DIM
    return pl.pallas_call(
        _dn_body,
        grid=(bsz, seq // DN_TILE),
        in_specs=[pl.BlockSpec((1, DN_TILE, uw), lambda b, t: (b, t, 0))]
        + [_const_spec(a.shape) for a in (conv_w, a_log, dt_bias, norm)],
        out_specs=pl.BlockSpec((1, DN_TILE, hw), lambda b, t: (b, t, 0)),
        out_shape=jax.ShapeDtypeStruct((bsz, seq, hw), BF16),
        scratch_shapes=[pltpu.VMEM((SUBLANES, 3 * hw), F32), pltpu.VMEM((DN_CHUNK, hw), F32)],
        compiler_params=_cparams("arbitrary", "arbitrary"),
        name="deltanet",
    )(u, conv_w, a_log, dt_bias, norm)


def _rel_bucket_table():
    qi = jnp.arange(BLOCK_Q)[:, None]
    kj = jnp.arange(2 * BLOCK_Q)[None, :]
    dist = BLOCK_Q + qi - kj
    band = (dist >= 0) & (dist < BLOCK_Q)
    d = jnp.maximum(dist, 0)
    max_exact = REL_BUCKETS // 2
    large = max_exact + (jnp.log(jnp.maximum(d, 1).astype(F32) / max_exact)
                         / math.log(REL_MAX_DIST / max_exact)
                         * (REL_BUCKETS - max_exact)).astype(jnp.int32)
    large = jnp.minimum(large, REL_BUCKETS - 1)
    return jnp.where(band, jnp.where(d < max_exact, d, large), -1).astype(jnp.int32)


def _block_diag(w):
    hh, n, _ = w.shape
    eye = jnp.eye(hh, dtype=w.dtype)
    return (eye[:, None, :, None] * w[:, :, None, :]).reshape(hh * n, hh * n)


def kernel(x, p, ffn1_norm, ffn1_w_gate, ffn1_w_up, ffn1_w_down, mix_norm, w_in, lru_conv_w, lru_conv_b, lru_w_a, lru_b_a, lru_w_x, lru_b_x, lru_lambda, attn_sinks, rel_bias, dn_conv_w, dn_a_log, dn_dt_bias, dn_norm, w_out, ffn2_norm, ffn2_w_gate, ffn2_w_up, ffn2_w_down, ple_norm, ple_w_gate, ple_w_proj, final_norm):
    bsz, seq, d = x.shape
    depth = w_in.shape[0]
    n = bsz * seq
    lru_w = lru_lambda.shape[1]
    q_w = attn_sinks.shape[1] * HEAD_DIM
    dn_w = DN_HEADS * HEAD_DIM
    kv_w = w_in.shape[2] - 2 * lru_w - q_w - 4 * dn_w - 2 * DN_HEADS
    q0 = 2 * lru_w
    main = q0 + q_w + kv_w + 4 * dn_w
    groups = ((2 * lru_w, F32), (q_w, BF16), (kv_w, F32), (6 * dn_w, F32))
    bucket = _rel_bucket_table()
    row = lambda v: v.reshape(1, -1).astype(F32)
    rep = lambda v: jnp.repeat(v, HEAD_DIM).reshape(1, -1)
    bf = lambda w: w.astype(BF16)
    ffn1 = (bf(ffn1_w_gate), bf(ffn1_w_up), bf(ffn1_w_down))
    ffn2 = (bf(ffn2_w_gate), bf(ffn2_w_up), bf(ffn2_w_down))
    wo, wpg, wpp = bf(w_out), bf(ple_w_gate), bf(ple_w_proj)
    p_rows = p.reshape(depth, n, -1)

    h = x.reshape(n, d)
    for l in range(depth):
        w_aug = bf(jnp.concatenate(
            [w_in[l, :, :q0],
             w_in[l, :, q0:q0 + q_w] * (HEAD_DIM ** -0.5 * LOG2E),
             w_in[l, :, q0 + q_w:main],
             jnp.repeat(w_in[l, :, main:main + DN_HEADS], HEAD_DIM, axis=1),
             jnp.repeat(w_in[l, :, main + DN_HEADS:], HEAD_DIM, axis=1)], axis=1))
        h, u_lru, u_q, u_kv, u_dn = _pre(h, l, row(ffn1_norm[l]), *ffn1, row(mix_norm[l]), w_aug, groups)

        w_ax = bf(jnp.concatenate([_block_diag(lru_w_a[l]), _block_diag(lru_w_x[l])], axis=1))
        b_ax = jnp.concatenate([lru_b_a[l], lru_b_x[l]]).reshape(1, -1)
        y_lru = _lru(u_lru.reshape(bsz, seq, -1), lru_conv_w[l], row(lru_conv_b[l]), w_ax, b_ax,
                     row(lru_lambda[l]))
        y_att = _attn(u_q.reshape(bsz, seq, -1), u_kv.reshape(bsz, seq, -1), bucket, rel_bias, attn_sinks[l])
        y_dn = _deltanet(u_dn.reshape(bsz, seq, -1), dn_conv_w[l], rep(dn_a_log[l]), rep(dn_dt_bias[l]),
                         jnp.tile(dn_norm[l], DN_HEADS).reshape(1, -1))

        h = _post(h, y_lru.reshape(n, -1), y_att.reshape(n, -1), y_dn.reshape(n, -1), p_rows, l,
                  wo, row(ffn2_norm[l]), *ffn2, row(ple_norm[l]), wpg, wpp, row(final_norm),
                  final=(l == depth - 1))
    return h.reshape(bsz, seq, d)
```

```python
import functools
import math

import jax
import jax.numpy as jnp
from jax import lax
from jax.experimental import pallas as pl
from jax.experimental.pallas import tpu as pltpu

F32 = jnp.float32
BF16 = jnp.bfloat16

EPS = 1e-6
HEAD_DIM = 64
LRU_C = 8.0
CONV_W = 4
ATT_GROUP = 4
BLOCK_Q = 128
REL_BUCKETS = 32
REL_MAX_DIST = 128
LOG2E = math.log2(math.e)
DN_CHUNK = 64
DN_HEADS = 4
SUBLANES = 8
LANES = 128
MXU_DIM = 256
VMEM_LIMIT = 56 * 1024 * 1024

TOKEN_TILE = 512
FFN_CHUNK = MXU_DIM
LRU_PIECES = 8
ATT_SLOTS = (2, 5, 8, 10)
DN_TILE = 512
DN_CUMSUM_ROWS = 256


def _cparams(*sem):
    return pltpu.CompilerParams(dimension_semantics=sem, vmem_limit_bytes=VMEM_LIMIT)


def _const_spec(shape):
    zeros = (0,) * len(shape)
    return pl.BlockSpec(shape, lambda *_: zeros, pipeline_mode=pl.Buffered(1))


def _layer_spec(shape, layer):
    idx = (layer,) + (0,) * (len(shape) - 1)
    return pl.BlockSpec((None,) + tuple(shape[1:]), lambda *_: idx, pipeline_mode=pl.Buffered(1))


def _rms(x, g):
    ms = jnp.mean(x * x, axis=-1, keepdims=True)
    return x * lax.rsqrt(ms + EPS) * g


def _mm(a, b):
    return jnp.dot(a, b, preferred_element_type=F32)


def _mm_nt(a, b):
    return lax.dot_general(a, b, (((1,), (1,)), ((), ())), preferred_element_type=F32)


def _swiglu_half_step(h, g_ref, wg_ref, wu_ref, wd_ref, side=()):
    xn = _rms(h, g_ref[...]).astype(BF16)
    acc = jnp.zeros(h.shape, F32)
    for c in range(wg_ref.shape[1] // FFN_CHUNK):
        cols = slice(c * FFN_CHUNK, (c + 1) * FFN_CHUNK)
        gate = _mm(xn, wg_ref[:, cols])
        up = _mm(xn, wu_ref[:, cols])
        act = (gate * jax.nn.sigmoid(gate) * up).astype(BF16)
        acc = acc + _mm(act, wd_ref[cols, :])
        for work in (side[c] if c < len(side) else ()):
            work()
    return h + 0.5 * acc


def _pre_body(h_ref, g1_ref, wg_ref, wu_ref, wd_ref, gm_ref, win_ref, ho_ref, *u_refs):
    h = _swiglu_half_step(h_ref[...], g1_ref, wg_ref, wu_ref, wd_ref)
    ho_ref[...] = h
    xn = _rms(h, gm_ref[...]).astype(BF16)
    off = 0
    for u_ref in u_refs:
        width = u_ref.shape[1]
        u_ref[...] = _mm(xn, win_ref[:, off:off + width]).astype(u_ref.dtype)
        off += width


def _pre(h, layer, g1, wg, wu, wd, gm, w_in, outs):
    n, d = h.shape
    row = lambda i: (i, 0)
    return pl.pallas_call(
        _pre_body,
        grid=(n // TOKEN_TILE,),
        in_specs=[pl.BlockSpec((TOKEN_TILE, d), row), _const_spec(g1.shape)]
        + [_layer_spec(a.shape, layer) for a in (wg, wu, wd)]
        + [_const_spec(gm.shape), _const_spec(w_in.shape)],
        out_specs=[pl.BlockSpec((TOKEN_TILE, d), row)]
        + [pl.BlockSpec((TOKEN_TILE, wd_), row) for wd_, _ in outs],
        out_shape=[jax.ShapeDtypeStruct((n, d), F32)]
        + [jax.ShapeDtypeStruct((n, wd_), dt) for wd_, dt in outs],
        compiler_params=_cparams("arbitrary"),
        name="pre",
    )(h, g1, wg, wu, wd, gm, w_in)


def _post_body(bucket_ref, relb_ref, sink_ref, h_ref, u0_ref, un_ref, q0_ref, qn_ref, kv0_ref, kvn_ref, kvp_ref,
               yc_ref, p_ref, cw_ref, cb_ref, wax_ref, bax_ref, lam_ref,
               wo_ref, g2_ref, wg_ref, wu_ref, wd_ref, gp_ref, wpg_ref, wpp_ref, fg_ref, o_ref,
               ylru_ref, yatt_ref, tail_ref, carry_ref, bias_ref, *, final, tiles_per_seq):
    step = pl.program_id(0)
    width = ylru_ref.shape[1]
    piece = TOKEN_TILE // LRU_PIECES
    n_blk = TOKEN_TILE // BLOCK_Q
    n_kv = bias_ref.shape[1]
    cw, cb, wax, bax = cw_ref[...], cb_ref[...], wax_ref[...], bax_ref[...]
    neg_lam = -lam_ref[...]
    softplus = jnp.maximum(neg_lam, 0.0) + jnp.log1p(jnp.exp(-jnp.abs(neg_lam)))
    sinks = [sink_ref[hd] * LOG2E for hd in range(n_kv * ATT_GROUP)]

    def lru_rows(u_ref, pc, st):
        rs = slice(pc * piece, (pc + 1) * piece)
        y, st[0], st[1] = _lru_rows(u_ref[rs, :width], u_ref[rs, width:], st[0], st[1], cw, cb, wax, bax, softplus)
        ylru_ref[rs, :] = y.astype(ylru_ref.dtype)

    def attn_rows(q_ref, kv_ref, prev_ref, sb, dup, first_has_prev):
        rs = slice(sb * BLOCK_Q, (sb + 1) * BLOCK_Q)
        if sb == 0:
            dup[0] = _attn_dup(prev_ref[...], n_kv)
        cur = _attn_dup(kv_ref[rs, :], n_kv)
        o = _attn_block(q_ref[rs, :], dup[0], cur, first_has_prev if sb == 0 else 1, bias_ref, sinks)
        yatt_ref[rs, :] = o.astype(yatt_ref.dtype)
        dup[0] = cur

    @pl.when(step == 0)
    def _():
        _attn_bias_init(bucket_ref, relb_ref, bias_ref)
        st = [jnp.zeros(tail_ref.shape, F32), jnp.zeros(carry_ref.shape, F32)]
        for pc in range(LRU_PIECES):
            lru_rows(u0_ref, pc, st)
        tail_ref[...], carry_ref[...] = st
        dup = [None]
        for sb in range(n_blk):
            attn_rows(q0_ref, kv0_ref, kv0_ref.at[0:BLOCK_Q, :], sb, dup, 0)

    y = jnp.concatenate([ylru_ref[...], yatt_ref[...], yc_ref[...]], axis=1)
    h = h_ref[...] + _mm(y, wo_ref[...])
    next_fresh = (step + 1) % tiles_per_seq == 0
    st = [jnp.where(next_fresh, 0.0, tail_ref[...]), jnp.where(next_fresh, 0.0, carry_ref[...])]
    next_has_prev = jnp.where(next_fresh, 0, 1)
    dup = [None]
    side = [[functools.partial(lru_rows, un_ref, pc, st)] for pc in range(LRU_PIECES)]
    side += [[] for _ in range(wg_ref.shape[1] // FFN_CHUNK - LRU_PIECES)]
    for sb, c in enumerate(ATT_SLOTS):
        side[c].append(functools.partial(attn_rows, qn_ref, kvn_ref, kvp_ref, sb, dup, next_has_prev))
    h = _swiglu_half_step(h, g2_ref, wg_ref, wu_ref, wd_ref, side)
    tail_ref[...], carry_ref[...] = st
    gate = jax.nn.sigmoid(_mm(_rms(h, gp_ref[...]).astype(BF16), wpg_ref[...]))
    h = h + gate * _mm(p_ref[...].astype(BF16), wpp_ref[...])
    if final:
        h = _rms(h, fg_ref[...])
    o_ref[...] = h


def _post(h, u_lru, u_q, u_kv, yc, p, layer, bucket, rel_bias, sinks, lru_consts, wo, g2, wg, wu, wd, gp, wpg, wpp, fg,
          final, tiles_per_seq):
    n, d = h.shape
    n_tiles = n // TOKEN_TILE
    n_blk = TOKEN_TILE // BLOCK_Q
    two_w, qw, kvw = u_lru.shape[1], u_q.shape[1], u_kv.shape[1]
    n_kv = kvw // (2 * HEAD_DIM)
    row = lambda i: (i, 0)
    nxt = lambda i: (jnp.minimum(i + 1, n_tiles - 1), 0)
    first = lambda i: (0, 0)
    stack = lambda a: _layer_spec(a.shape, layer)
    gain = lambda a: _const_spec(a.shape)
    once = lambda w_: pl.BlockSpec((TOKEN_TILE, w_), first, pipeline_mode=pl.Buffered(1))
    return pl.pallas_call(
        functools.partial(_post_body, final=final, tiles_per_seq=tiles_per_seq),
        grid=(n_tiles,),
        in_specs=[gain(bucket), pl.BlockSpec(memory_space=pltpu.SMEM), pl.BlockSpec(memory_space=pltpu.SMEM),
                  pl.BlockSpec((TOKEN_TILE, d), row),
                  once(two_w), pl.BlockSpec((TOKEN_TILE, two_w), nxt),
                  once(qw), pl.BlockSpec((TOKEN_TILE, qw), nxt),
                  once(kvw), pl.BlockSpec((TOKEN_TILE, kvw), nxt),
                  pl.BlockSpec((BLOCK_Q, kvw), lambda i: (jnp.minimum(i + 1, n_tiles - 1) * n_blk - 1, 0)),
                  pl.BlockSpec((TOKEN_TILE, yc.shape[1]), row),
                  pl.BlockSpec((None, TOKEN_TILE, p.shape[2]), lambda i: (layer, i, 0))]
        + [gain(a) for a in lru_consts]
        + [stack(wo), gain(g2), stack(wg), stack(wu), stack(wd), gain(gp), stack(wpg), stack(wpp), gain(fg)],
        out_specs=pl.BlockSpec((TOKEN_TILE, d), row),
        out_shape=jax.ShapeDtypeStruct((n, d), F32),
        scratch_shapes=[pltpu.VMEM((TOKEN_TILE, two_w // 2), BF16), pltpu.VMEM((TOKEN_TILE, qw), BF16),
                        pltpu.VMEM((SUBLANES, two_w // 2), F32), pltpu.VMEM((1, two_w // 2), F32),
                        pltpu.VMEM((2, n_kv, ATT_GROUP, BLOCK_Q, 2 * BLOCK_Q), F32)],
        compiler_params=_cparams("arbitrary"),
        name="post",
    )(bucket, rel_bias, sinks, h, u_lru, u_lru, u_q, u_q, u_kv, u_kv, u_kv, yc, p, *lru_consts,
      wo, g2, wg, wu, wd, gp, wpg, wpp, fg)


def _conv_rows(x, tail, w):
    rows, ch = x.shape
    groups = rows // SUBLANES
    ext = jnp.concatenate([tail, x], axis=0).reshape(groups + 1, SUBLANES, ch)
    sub = lax.broadcasted_iota(jnp.int32, (groups, SUBLANES, ch), 1)
    y = x * w[CONV_W - 1:CONV_W]
    for back in range(1, CONV_W):
        rot = pltpu.roll(ext, back, 1)
        shifted = jnp.where(sub >= back, rot[1:], rot[:-1]).reshape(rows, ch)
        y = y + shifted * w[CONV_W - 1 - back:CONV_W - back]
    return y


def _causal_conv(x, tail_ref, w):
    y = _conv_rows(x, tail_ref[...], w)
    tail_ref[...] = x[x.shape[0] - SUBLANES:]
    return y


def _lru_rows(x, gate_in, tail, carry, cw, cb, wax, bax, softplus):
    rows, width = x.shape
    groups = rows // SUBLANES
    xr = _conv_rows(x, tail, cw) + cb
    gates = jax.nn.sigmoid(_mm(xr.astype(BF16), wax) + bax)
    r, i = gates[:, :width], gates[:, width:]
    log_a = -LRU_C * r * softplus
    a = jnp.exp(log_a)
    b = jnp.sqrt(-jnp.tanh(log_a) * (a * a + 1.0)) * (i * xr)
    a = a.reshape(groups, SUBLANES, width)
    b = b.reshape(groups, SUBLANES, width)
    sub = lax.broadcasted_iota(jnp.int32, a.shape, 1)
    for d in (1, 2, 4):
        ok = sub >= d
        b = jnp.where(ok, a * pltpu.roll(b, d, 1) + b, b)
        a = jnp.where(ok, a * pltpu.roll(a, d, 1), a)
    outs = []
    for g in range(groups):
        hg = a[g] * carry + b[g]
        outs.append(hg)
        carry = hg[SUBLANES - 1:SUBLANES]
    hs = jnp.concatenate(outs, axis=0)
    return jax.nn.gelu(gate_in) * hs, x[rows - SUBLANES:], carry


def _attn_bias_init(bucket_ref, relb_ref, bias_ref):
    bucket = bucket_ref[...]
    in_prev = lax.broadcasted_iota(jnp.int32, bucket.shape, 1) < BLOCK_Q
    for head in range(bias_ref.shape[1] * ATT_GROUP):
        tbl = jnp.full(bucket.shape, -jnp.inf, F32)
        for bk in range(REL_BUCKETS):
            tbl = jnp.where(bucket == bk, relb_ref[bk, head] * LOG2E, tbl)
        g, j = divmod(head, ATT_GROUP)
        bias_ref[1, g, j] = tbl
        bias_ref[0, g, j] = jnp.where(in_prev, -jnp.inf, tbl)


def _attn_dup(kvb, n_kv):
    lane = lax.broadcasted_iota(jnp.int32, (BLOCK_Q, LANES), 1)
    kw = n_kv * HEAD_DIM
    out = []
    for g in range(n_kv):
        tile = g // 2
        k2 = kvb[:, tile * LANES:(tile + 1) * LANES]
        v2 = kvb[:, kw + tile * LANES:kw + (tile + 1) * LANES]
        keep = (lane < HEAD_DIM) if g % 2 == 0 else (lane >= HEAD_DIM)
        out.append((jnp.where(keep, k2, pltpu.roll(k2, HEAD_DIM, 1)).astype(BF16),
                    jnp.where(keep, v2, pltpu.roll(v2, HEAD_DIM, 1)).astype(BF16)))
    return out


def _attn_block(q, dup_prev, dup_cur, has_prev, bias_ref, sinks):
    lane = lax.broadcasted_iota(jnp.int32, (BLOCK_Q, LANES), 1)
    ones = jnp.ones((2 * BLOCK_Q, LANES), BF16)
    zero = jnp.zeros((), BF16)
    cols = []
    for g in range(len(dup_cur)):
        parts = []
        for pr in range(ATT_GROUP // 2):
            c0 = (g * ATT_GROUP // 2 + pr) * LANES
            qp = q[:, c0:c0 + LANES]
            parts.append(jnp.where(lane < HEAD_DIM, qp, zero))
            parts.append(jnp.where(lane >= HEAD_DIM, qp, zero))
        keys = jnp.concatenate([dup_prev[g][0], dup_cur[g][0]], axis=0)
        s = _mm_nt(jnp.concatenate(parts, axis=0), keys)
        e_g, m_g = [], []
        for j in range(ATT_GROUP):
            sj = s[j * BLOCK_Q:(j + 1) * BLOCK_Q] + bias_ref[has_prev, g, j]
            m = jnp.maximum(jnp.max(sj, axis=1, keepdims=True), sinks[g * ATT_GROUP + j])
            e_g.append(jnp.exp2(sj - m).astype(BF16))
            m_g.append(m)
        v_aug = jnp.concatenate([jnp.concatenate([dup_prev[g][1], dup_cur[g][1]], axis=0), ones], axis=1)
        oa = _mm(jnp.concatenate(e_g, axis=0), v_aug)
        outs = []
        for j in range(ATT_GROUP):
            oj = oa[j * BLOCK_Q:(j + 1) * BLOCK_Q]
            den = oj[:, LANES:] + jnp.exp2(sinks[g * ATT_GROUP + j] - m_g[j])
            outs.append(oj[:, :LANES] / den)
        for pr in range(ATT_GROUP // 2):
            cols.append(jnp.where(lane < HEAD_DIM, outs[2 * pr], outs[2 * pr + 1]))
    return jnp.concatenate(cols, axis=1)


def _split3(x):
    hi = x.astype(BF16)
    r1 = x - hi.astype(F32)
    mid = r1.astype(BF16)
    lo = (r1 - mid.astype(F32)).astype(BF16)
    return hi, mid, lo


def _mm_const_lhs(c, x):
    hi, mid, lo = _split3(x)
    return _mm(c, hi) + _mm(c, mid) + _mm(c, lo)


def _bd(x, bd_mask):
    return jnp.where(bd_mask, jnp.concatenate([x] * DN_HEADS, axis=0), jnp.zeros((), x.dtype))


def _head_transpose(x):
    xt = x.T
    return jnp.concatenate([xt[hd * DN_CHUNK:(hd + 1) * DN_CHUNK] for hd in range(DN_HEADS)], axis=1)


def _dn_body(u_ref, cw_ref, alog_ref, dtb_ref, nrm_ref, o_ref, tail_ref, s_ref):
    hw = DN_HEADS * HEAD_DIM
    rows = o_ref.shape[1]
    c = DN_CHUNK

    @pl.when(pl.program_id(1) == 0)
    def _():
        tail_ref[...] = jnp.zeros_like(tail_ref)
        s_ref[...] = jnp.zeros_like(s_ref)

    r_i = lax.broadcasted_iota(jnp.int32, (c, hw), 0)
    l_j = lax.broadcasted_iota(jnp.int32, (c, hw), 1) % c
    diag = r_i == l_j
    low_incl = r_i >= l_j
    low_strict = r_i > l_j
    bd_r = lax.broadcasted_iota(jnp.int32, (hw, hw), 0) // c
    bd_c = lax.broadcasted_iota(jnp.int32, (hw, hw), 1) // c
    bd_mask = bd_r == bd_c
    ones_bd = jnp.where(bd_mask, 1.0, 0.0).astype(BF16)
    t_r = lax.broadcasted_iota(jnp.int32, (DN_CUMSUM_ROWS, DN_CUMSUM_ROWS), 0)
    t_c = lax.broadcasted_iota(jnp.int32, (DN_CUMSUM_ROWS, DN_CUMSUM_ROWS), 1)
    chunk_tril = jnp.where((t_r // c == t_c // c) & (t_r >= t_c), 1.0, 0.0).astype(BF16)

    def group_sum(x):
        return _mm(x.astype(BF16), ones_bd)

    u = u_ref[0]
    qkv = _causal_conv(u[:, :3 * hw], tail_ref, cw_ref[...])
    qkv = qkv * jax.nn.sigmoid(qkv)
    q, k, v = qkv[:, :hw], qkv[:, hw:2 * hw], qkv[:, 2 * hw:]
    z = u[:, 3 * hw:4 * hw]
    beta = jax.nn.sigmoid(u[:, 4 * hw:5 * hw])
    a = u[:, 5 * hw:6 * hw] + dtb_ref[...]
    g = -jnp.exp(alog_ref[...]) * (jnp.maximum(a, 0.0) + jnp.log(1.0 + jnp.exp(-jnp.abs(a))))
    q = q * (lax.rsqrt(group_sum(q * q) + EPS) * (HEAD_DIM ** -0.5))
    k = k * lax.rsqrt(group_sum(k * k) + EPS)
    gcum = jnp.concatenate(
        [_mm_const_lhs(chunk_tril, g[r0:r0 + DN_CUMSUM_ROWS]) for r0 in range(0, rows, DN_CUMSUM_ROWS)], axis=0)

    def pmm(a_pk, b_pk):
        return _mm(a_pk.astype(BF16), _bd(b_pk.astype(BF16), bd_mask))

    ch = range(rows // c)
    sl = [slice(i * c, (i + 1) * c) for i in ch]
    qc, kc, gc = [q[s] for s in sl], [k[s] for s in sl], [gcum[s] for s in sl]
    g_last = [gc[i][c - 1:c] for i in ch]
    eg = [jnp.exp(gc[i]) for i in ch]
    k_beta = [kc[i] * beta[sl[i]] for i in ch]
    v_beta = [v[sl[i]] * beta[sl[i]] for i in ch]
    g_row = [jnp.sum(jnp.where(diag, gc[i], 0.0), axis=0, keepdims=True) for i in ch]
    decay = [jnp.exp(jnp.where(low_incl, gc[i] - g_row[i], -jnp.inf)) for i in ch]
    kq = [_mm_nt(jnp.concatenate([k_beta[i], qc[i]], axis=0).astype(BF16), _bd(kc[i].astype(BF16), bd_mask))
          for i in ch]
    attn = [kq[i][c:] * decay[i] for i in ch]
    m1 = [-jnp.where(low_strict, kq[i][:c] * decay[i], 0.0) for i in ch]
    m2 = [pmm(m1[i], m1[i]) for i in ch]
    m34 = [pmm(jnp.concatenate([m1[i], m2[i]], axis=0), m2[i]) for i in ch]
    psum = [jnp.where(diag, 1.0, 0.0) + m1[i] + m2[i] + m34[i][:c] for i in ch]
    mp = [m34[i][c:] for i in ch]
    for _ in range(3):
        both = [pmm(jnp.concatenate([mp[i], psum[i]], axis=0), mp[i]) for i in ch]
        psum = [psum[i] + both[i][c:] for i in ch]
        mp = [both[i][:c] for i in ch]
    tb = [(psum[i] + pmm(psum[i], mp[i])).astype(BF16) for i in ch]
    u_c = [_mm(tb[i], _bd(v_beta[i].astype(BF16), bd_mask)).astype(BF16) for i in ch]
    w_c = [_mm(tb[i], _bd((k_beta[i] * eg[i]).astype(BF16), bd_mask)).astype(BF16) for i in ch]
    lhs = [jnp.concatenate([_head_transpose(kc[i] * jnp.exp(g_last[i] - gc[i])), attn[i]], axis=0).astype(BF16)
           for i in ch]
    wu = [_mm(lhs[i], jnp.concatenate([_bd(w_c[i], bd_mask), _bd(u_c[i], bd_mask)], axis=1)) for i in ch]
    q_eff = [qc[i] * eg[i] - wu[i][c:, :hw] for i in ch]
    state = s_ref[...]
    outs = []
    for i in ch:
        res = pmm(jnp.concatenate([wu[i][:c, :hw], q_eff[i]], axis=0), state)
        outs.append(res[c:] + wu[i][c:, hw:])
        state = state * jnp.exp(g_last[i]) - res[:c] + wu[i][:c, hw:]
    s_ref[...] = state
    o = jnp.concatenate(outs, axis=0)
    ms = group_sum(o * o) * (1.0 / HEAD_DIM)
    o_ref[0] = (o * lax.rsqrt(ms + EPS) * nrm_ref[...] * (z * jax.nn.sigmoid(z))).astype(o_ref.dtype)


def _deltanet(u, conv_w, a_log, dt_bias, norm):
    bsz, seq, uw = u.shape
    hw = DN_HEADS * HEAD_DIM
    return pl.pallas_call(
        _dn_body,
        grid=(bsz, seq // DN_TILE),
        in_specs=[pl.BlockSpec((1, DN_TILE, uw), lambda b, t: (b, t, 0))]
        + [_const_spec(a.shape) for a in (conv_w, a_log, dt_bias, norm)],
        out_specs=pl.BlockSpec((1, DN_TILE, hw), lambda b, t: (b, t, 0)),
        out_shape=jax.ShapeDtypeStruct((bsz, seq, hw), BF16),
        scratch_shapes=[pltpu.VMEM((SUBLANES, 3 * hw), F32), pltpu.VMEM((DN_CHUNK, hw), F32)],
        compiler_params=_cparams("arbitrary", "arbitrary"),
        name="deltanet",
    )(u, conv_w, a_log, dt_bias, norm)


def _rel_bucket_table():
    qi = jnp.arange(BLOCK_Q)[:, None]
    kj = jnp.arange(2 * BLOCK_Q)[None, :]
    dist = BLOCK_Q + qi - kj
    band = (dist >= 0) & (dist < BLOCK_Q)
    d = jnp.maximum(dist, 0)
    max_exact = REL_BUCKETS // 2
    large = max_exact + (jnp.log(jnp.maximum(d, 1).astype(F32) / max_exact)
                         / math.log(REL_MAX_DIST / max_exact)
                         * (REL_BUCKETS - max_exact)).astype(jnp.int32)
    large = jnp.minimum(large, REL_BUCKETS - 1)
    return jnp.where(band, jnp.where(d < max_exact, d, large), -1).astype(jnp.int32)


def _block_diag(w):
    hh, n, _ = w.shape
    eye = jnp.eye(hh, dtype=w.dtype)
    return (eye[:, None, :, None] * w[:, :, None, :]).reshape(hh * n, hh * n)


def kernel(x, p, ffn1_norm, ffn1_w_gate, ffn1_w_up, ffn1_w_down, mix_norm, w_in, lru_conv_w, lru_conv_b, lru_w_a, lru_b_a, lru_w_x, lru_b_x, lru_lambda, attn_sinks, rel_bias, dn_conv_w, dn_a_log, dn_dt_bias, dn_norm, w_out, ffn2_norm, ffn2_w_gate, ffn2_w_up, ffn2_w_down, ple_norm, ple_w_gate, ple_w_proj, final_norm):
    bsz, seq, d = x.shape
    depth = w_in.shape[0]
    n = bsz * seq
    lru_w = lru_lambda.shape[1]
    q_w = attn_sinks.shape[1] * HEAD_DIM
    dn_w = DN_HEADS * HEAD_DIM
    kv_w = w_in.shape[2] - 2 * lru_w - q_w - 4 * dn_w - 2 * DN_HEADS
    q0 = 2 * lru_w
    main = q0 + q_w + kv_w + 4 * dn_w
    groups = ((2 * lru_w, F32), (q_w, BF16), (kv_w, F32), (6 * dn_w, F32))
    bucket = _rel_bucket_table()
    row = lambda v: v.reshape(1, -1).astype(F32)
    rep = lambda v: jnp.repeat(v, HEAD_DIM).reshape(1, -1)
    bf = lambda w: w.astype(BF16)
    ffn1 = (bf(ffn1_w_gate), bf(ffn1_w_up), bf(ffn1_w_down))
    ffn2 = (bf(ffn2_w_gate), bf(ffn2_w_up), bf(ffn2_w_down))
    wo, wpg, wpp = bf(w_out), bf(ple_w_gate), bf(ple_w_proj)
    p_rows = p.reshape(depth, n, -1)

    h = x.reshape(n, d)
    for l in range(depth):
        w_aug = bf(jnp.concatenate(
            [w_in[l, :, :q0],
             w_in[l, :, q0:q0 + q_w] * (HEAD_DIM ** -0.5 * LOG2E),
             w_in[l, :, q0 + q_w:main],
             jnp.repeat(w_in[l, :, main:main + DN_HEADS], HEAD_DIM, axis=1),
             jnp.repeat(w_in[l, :, main + DN_HEADS:], HEAD_DIM, axis=1)], axis=1))
        h, u_lru, u_q, u_kv, u_dn = _pre(h, l, row(ffn1_norm[l]), *ffn1, row(mix_norm[l]), w_aug, groups)

        w_ax = bf(jnp.concatenate([_block_diag(lru_w_a[l]), _block_diag(lru_w_x[l])], axis=1))
        b_ax = jnp.concatenate([lru_b_a[l], lru_b_x[l]]).reshape(1, -1)
        lru_consts = (lru_conv_w[l], row(lru_conv_b[l]), w_ax, b_ax, row(lru_lambda[l]))
        y_dn = _deltanet(u_dn.reshape(bsz, seq, -1), dn_conv_w[l], rep(dn_a_log[l]), rep(dn_dt_bias[l]),
                         jnp.tile(dn_norm[l], DN_HEADS).reshape(1, -1))

        h = _post(h, u_lru, u_q, u_kv, y_dn.reshape(n, -1), p_rows, l, bucket, rel_bias, attn_sinks[l], lru_consts,
                  wo, row(ffn2_norm[l]), *ffn2, row(ple_norm[l]), wpg, wpp, row(final_norm),
                  final=(l == depth - 1), tiles_per_seq=seq // TOKEN_TILE)
    return h.reshape(bsz, seq, d)
```

```python
import functools
import math

import jax
import jax.numpy as jnp
from jax import lax
from jax.experimental import pallas as pl
from jax.experimental.pallas import tpu as pltpu

F32 = jnp.float32
BF16 = jnp.bfloat16

EPS = 1e-6
HEAD_DIM = 64
LRU_C = 8.0
CONV_W = 4
ATT_GROUP = 4
BLOCK_Q = 128
REL_BUCKETS = 32
REL_MAX_DIST = 128
LOG2E = math.log2(math.e)
DN_CHUNK = 64
DN_HEADS = 4
SUBLANES = 8
LANES = 128
MXU_DIM = 256
VMEM_LIMIT = 56 * 1024 * 1024

TOKEN_TILE = 512
FFN_CHUNK = MXU_DIM
LRU_PIECES = 8
ATT_SLOTS = (8, 10)
DN_TILE = 512
DN_CUMSUM_ROWS = 256


def _cparams(*sem):
    return pltpu.CompilerParams(dimension_semantics=sem, vmem_limit_bytes=VMEM_LIMIT)


def _const_spec(shape):
    zeros = (0,) * len(shape)
    return pl.BlockSpec(shape, lambda *_: zeros, pipeline_mode=pl.Buffered(1))


def _layer_spec(shape, layer):
    idx = (layer,) + (0,) * (len(shape) - 1)
    return pl.BlockSpec((None,) + tuple(shape[1:]), lambda *_: idx, pipeline_mode=pl.Buffered(1))


def _rms(x, g):
    ms = jnp.mean(x * x, axis=-1, keepdims=True)
    return x * lax.rsqrt(ms + EPS) * g


def _mm(a, b):
    return jnp.dot(a, b, preferred_element_type=F32)


def _mm_nt(a, b):
    return lax.dot_general(a, b, (((1,), (1,)), ((), ())), preferred_element_type=F32)


def _swiglu_half_step(h, g_ref, wg_ref, wu_ref, wd_ref, side=()):
    xn = _rms(h, g_ref[...]).astype(BF16)
    acc = jnp.zeros(h.shape, F32)
    for c in range(wg_ref.shape[1] // FFN_CHUNK):
        cols = slice(c * FFN_CHUNK, (c + 1) * FFN_CHUNK)
        gate = _mm(xn, wg_ref[:, cols])
        up = _mm(xn, wu_ref[:, cols])
        act = (gate * jax.nn.sigmoid(gate) * up).astype(BF16)
        acc = acc + _mm(act, wd_ref[cols, :])
        for work in (side[c] if c < len(side) else ()):
            work()
    return h + 0.5 * acc


def _pre_body(h_ref, g1_ref, wg_ref, wu_ref, wd_ref, gm_ref, win_ref, ho_ref, *u_refs):
    h = _swiglu_half_step(h_ref[...], g1_ref, wg_ref, wu_ref, wd_ref)
    ho_ref[...] = h
    xn = _rms(h, gm_ref[...]).astype(BF16)
    off = 0
    for u_ref in u_refs:
        width = u_ref.shape[1]
        u_ref[...] = _mm(xn, win_ref[:, off:off + width]).astype(u_ref.dtype)
        off += width


def _pre(h, layer, g1, wg, wu, wd, gm, w_in, outs):
    n, d = h.shape
    row = lambda i: (i, 0)
    return pl.pallas_call(
        _pre_body,
        grid=(n // TOKEN_TILE,),
        in_specs=[pl.BlockSpec((TOKEN_TILE, d), row), _const_spec(g1.shape)]
        + [_layer_spec(a.shape, layer) for a in (wg, wu, wd)]
        + [_const_spec(gm.shape), _const_spec(w_in.shape)],
        out_specs=[pl.BlockSpec((TOKEN_TILE, d), row)]
        + [pl.BlockSpec((TOKEN_TILE, wd_), row) for wd_, _ in outs],
        out_shape=[jax.ShapeDtypeStruct((n, d), F32)]
        + [jax.ShapeDtypeStruct((n, wd_), dt) for wd_, dt in outs],
        compiler_params=_cparams("arbitrary"),
        name="pre",
    )(h, g1, wg, wu, wd, gm, w_in)


def _post_body(bucket_ref, relb_ref, sink_ref, h_ref, u0_ref, un_ref, q0_ref, qn_ref, kv0_ref, kvn_ref, kvp_ref,
               yc_ref, p_ref, cw_ref, cb_ref, wax_ref, bax_ref, lam_ref,
               wo_ref, g2_ref, wg_ref, wu_ref, wd_ref, gp_ref, wpg_ref, wpp_ref, fg_ref, o_ref,
               ylru_ref, yatt_ref, tail_ref, carry_ref, bias_ref, *, final, tiles_per_seq):
    step = pl.program_id(0)
    width = ylru_ref.shape[1]
    piece = TOKEN_TILE // LRU_PIECES
    n_blk = TOKEN_TILE // BLOCK_Q
    n_kv = bias_ref.shape[1]
    cw, cb, wax, bax = cw_ref[...], cb_ref[...], wax_ref[...], bax_ref[...]
    neg_lam = -lam_ref[...]
    softplus = jnp.maximum(neg_lam, 0.0) + jnp.log1p(jnp.exp(-jnp.abs(neg_lam)))
    sinks = [sink_ref[hd] * LOG2E for hd in range(n_kv * ATT_GROUP)]

    def lru_rows(u_ref, pc, st):
        rs = slice(pc * piece, (pc + 1) * piece)
        y, st[0], st[1] = _lru_rows(u_ref[rs, :width], u_ref[rs, width:], st[0], st[1], cw, cb, wax, bax, softplus)
        ylru_ref[rs, :] = y.astype(ylru_ref.dtype)

    def attn_rows(q_ref, kv_ref, prev_ref, sbs, dup, first_has_prev):
        blocks = []
        for sb in sbs:
            rs = slice(sb * BLOCK_Q, (sb + 1) * BLOCK_Q)
            if sb == 0:
                dup[0] = _attn_dup(prev_ref[...], n_kv)
            cur = _attn_dup(kv_ref[rs, :], n_kv)
            blocks.append((q_ref[rs, :], dup[0], cur, first_has_prev if sb == 0 else 1))
            dup[0] = cur
        for sb, o in zip(sbs, _attn_blocks(blocks, bias_ref, sinks)):
            yatt_ref[sb * BLOCK_Q:(sb + 1) * BLOCK_Q, :] = o.astype(yatt_ref.dtype)

    @pl.when(step == 0)
    def _():
        _attn_bias_init(bucket_ref, relb_ref, bias_ref)
        st = [jnp.zeros(tail_ref.shape, F32), jnp.zeros(carry_ref.shape, F32)]
        for pc in range(LRU_PIECES):
            lru_rows(u0_ref, pc, st)
        tail_ref[...], carry_ref[...] = st
        dup = [None]
        attn_rows(q0_ref, kv0_ref, kv0_ref.at[0:BLOCK_Q, :], tuple(range(n_blk)), dup, 0)

    y = jnp.concatenate([ylru_ref[...], yatt_ref[...], yc_ref[...]], axis=1)
    h = h_ref[...] + _mm(y, wo_ref[...])
    next_fresh = (step + 1) % tiles_per_seq == 0
    st = [jnp.where(next_fresh, 0.0, tail_ref[...]), jnp.where(next_fresh, 0.0, carry_ref[...])]
    next_has_prev = jnp.where(next_fresh, 0, 1)
    dup = [None]
    side = [[functools.partial(lru_rows, un_ref, pc, st)] for pc in range(LRU_PIECES)]
    side += [[] for _ in range(wg_ref.shape[1] // FFN_CHUNK - LRU_PIECES)]
    per_slot = n_blk // len(ATT_SLOTS)
    for k_, c in enumerate(ATT_SLOTS):
        sbs = tuple(range(k_ * per_slot, (k_ + 1) * per_slot))
        side[c].append(functools.partial(attn_rows, qn_ref, kvn_ref, kvp_ref, sbs, dup, next_has_prev))
    h = _swiglu_half_step(h, g2_ref, wg_ref, wu_ref, wd_ref, side)
    tail_ref[...], carry_ref[...] = st
    gate = jax.nn.sigmoid(_mm(_rms(h, gp_ref[...]).astype(BF16), wpg_ref[...]))
    h = h + gate * _mm(p_ref[...].astype(BF16), wpp_ref[...])
    if final:
        h = _rms(h, fg_ref[...])
    o_ref[...] = h


def _post(h, u_lru, u_q, u_kv, yc, p, layer, bucket, rel_bias, sinks, lru_consts, wo, g2, wg, wu, wd, gp, wpg, wpp, fg,
          final, tiles_per_seq):
    n, d = h.shape
    n_tiles = n // TOKEN_TILE
    n_blk = TOKEN_TILE // BLOCK_Q
    two_w, qw, kvw = u_lru.shape[1], u_q.shape[1], u_kv.shape[1]
    n_kv = kvw // (2 * HEAD_DIM)
    row = lambda i: (i, 0)
    nxt = lambda i: (jnp.minimum(i + 1, n_tiles - 1), 0)
    first = lambda i: (0, 0)
    stack = lambda a: _layer_spec(a.shape, layer)
    gain = lambda a: _const_spec(a.shape)
    once = lambda w_: pl.BlockSpec((TOKEN_TILE, w_), first, pipeline_mode=pl.Buffered(1))
    return pl.pallas_call(
        functools.partial(_post_body, final=final, tiles_per_seq=tiles_per_seq),
        grid=(n_tiles,),
        in_specs=[gain(bucket), pl.BlockSpec(memory_space=pltpu.SMEM), pl.BlockSpec(memory_space=pltpu.SMEM),
                  pl.BlockSpec((TOKEN_TILE, d), row),
                  once(two_w), pl.BlockSpec((TOKEN_TILE, two_w), nxt),
                  once(qw), pl.BlockSpec((TOKEN_TILE, qw), nxt),
                  once(kvw), pl.BlockSpec((TOKEN_TILE, kvw), nxt),
                  pl.BlockSpec((BLOCK_Q, kvw), lambda i: (jnp.minimum(i + 1, n_tiles - 1) * n_blk - 1, 0)),
                  pl.BlockSpec((TOKEN_TILE, yc.shape[1]), row),
                  pl.BlockSpec((None, TOKEN_TILE, p.shape[2]), lambda i: (layer, i, 0))]
        + [gain(a) for a in lru_consts]
        + [stack(wo), gain(g2), stack(wg), stack(wu), stack(wd), gain(gp), stack(wpg), stack(wpp), gain(fg)],
        out_specs=pl.BlockSpec((TOKEN_TILE, d), row),
        out_shape=jax.ShapeDtypeStruct((n, d), F32),
        scratch_shapes=[pltpu.VMEM((TOKEN_TILE, two_w // 2), BF16), pltpu.VMEM((TOKEN_TILE, qw), BF16),
                        pltpu.VMEM((SUBLANES, two_w // 2), F32), pltpu.VMEM((1, two_w // 2), F32),
                        pltpu.VMEM((2, n_kv, ATT_GROUP, BLOCK_Q, 2 * BLOCK_Q), F32)],
        compiler_params=_cparams("arbitrary"),
        name="post",
    )(bucket, rel_bias, sinks, h, u_lru, u_lru, u_q, u_q, u_kv, u_kv, u_kv, yc, p, *lru_consts,
      wo, g2, wg, wu, wd, gp, wpg, wpp, fg)


def _conv_rows(x, tail, w):
    rows, ch = x.shape
    groups = rows // SUBLANES
    ext = jnp.concatenate([tail, x], axis=0).reshape(groups + 1, SUBLANES, ch)
    sub = lax.broadcasted_iota(jnp.int32, (groups, SUBLANES, ch), 1)
    y = x * w[CONV_W - 1:CONV_W]
    for back in range(1, CONV_W):
        rot = pltpu.roll(ext, back, 1)
        shifted = jnp.where(sub >= back, rot[1:], rot[:-1]).reshape(rows, ch)
        y = y + shifted * w[CONV_W - 1 - back:CONV_W - back]
    return y


def _lru_rows(x, gate_in, tail, carry, cw, cb, wax, bax, softplus):
    rows, width = x.shape
    groups = rows // SUBLANES
    xr = _conv_rows(x, tail, cw) + cb
    gates = jax.nn.sigmoid(_mm(xr.astype(BF16), wax) + bax)
    r, i = gates[:, :width], gates[:, width:]
    log_a = -LRU_C * r * softplus
    a = jnp.exp(log_a)
    b = jnp.sqrt(-jnp.tanh(log_a) * (a * a + 1.0)) * (i * xr)
    a = a.reshape(groups, SUBLANES, width)
    b = b.reshape(groups, SUBLANES, width)
    sub = lax.broadcasted_iota(jnp.int32, a.shape, 1)
    for d in (1, 2, 4):
        ok = sub >= d
        b = jnp.where(ok, a * pltpu.roll(b, d, 1) + b, b)
        a = jnp.where(ok, a * pltpu.roll(a, d, 1), a)
    outs = []
    for g in range(groups):
        hg = a[g] * carry + b[g]
        outs.append(hg)
        carry = hg[SUBLANES - 1:SUBLANES]
    hs = jnp.concatenate(outs, axis=0)
    return jax.nn.gelu(gate_in) * hs, x[rows - SUBLANES:], carry


def _attn_bias_init(bucket_ref, relb_ref, bias_ref):
    bucket = bucket_ref[...]
    in_prev = lax.broadcasted_iota(jnp.int32, bucket.shape, 1) < BLOCK_Q
    for head in range(bias_ref.shape[1] * ATT_GROUP):
        tbl = jnp.full(bucket.shape, -jnp.inf, F32)
        for bk in range(REL_BUCKETS):
            tbl = jnp.where(bucket == bk, relb_ref[bk, head] * LOG2E, tbl)
        g, j = divmod(head, ATT_GROUP)
        bias_ref[1, g, j] = tbl
        bias_ref[0, g, j] = jnp.where(in_prev, -jnp.inf, tbl)


def _attn_dup(kvb, n_kv):
    lane = lax.broadcasted_iota(jnp.int32, (BLOCK_Q, LANES), 1)
    kw = n_kv * HEAD_DIM
    out = []
    for g in range(n_kv):
        tile = g // 2
        k2 = kvb[:, tile * LANES:(tile + 1) * LANES]
        v2 = kvb[:, kw + tile * LANES:kw + (tile + 1) * LANES]
        keep = (lane < HEAD_DIM) if g % 2 == 0 else (lane >= HEAD_DIM)
        out.append((jnp.where(keep, k2, pltpu.roll(k2, HEAD_DIM, 1)).astype(BF16),
                    jnp.where(keep, v2, pltpu.roll(v2, HEAD_DIM, 1)).astype(BF16)))
    return out


def _attn_blocks(blocks, bias_ref, sinks):
    lane = lax.broadcasted_iota(jnp.int32, (BLOCK_Q, LANES), 1)
    ones = jnp.ones((2 * BLOCK_Q, LANES), BF16)
    zero = jnp.zeros((), BF16)
    n_kv = len(blocks[0][2])
    probs = [(b, g) for b in range(len(blocks)) for g in range(n_kv)]
    scores = []
    for b, g in probs:
        q, dup_prev, dup_cur, _ = blocks[b]
        parts = []
        for pr in range(ATT_GROUP // 2):
            c0 = (g * ATT_GROUP // 2 + pr) * LANES
            qp = q[:, c0:c0 + LANES]
            parts.append(jnp.where(lane < HEAD_DIM, qp, zero))
            parts.append(jnp.where(lane >= HEAD_DIM, qp, zero))
        keys = jnp.concatenate([dup_prev[g][0], dup_cur[g][0]], axis=0)
        scores.append(_mm_nt(jnp.concatenate(parts, axis=0), keys))
    es, ms = [], []
    for (b, g), s in zip(probs, scores):
        has_prev = blocks[b][3]
        e_g, m_g = [], []
        for j in range(ATT_GROUP):
            sj = s[j * BLOCK_Q:(j + 1) * BLOCK_Q] + bias_ref[has_prev, g, j]
            m = jnp.maximum(jnp.max(sj, axis=1, keepdims=True), sinks[g * ATT_GROUP + j])
            e_g.append(jnp.exp2(sj - m).astype(BF16))
            m_g.append(m)
        es.append(jnp.concatenate(e_g, axis=0))
        ms.append(m_g)
    oas = []
    for (b, g), e in zip(probs, es):
        _, dup_prev, dup_cur, _ = blocks[b]
        v_aug = jnp.concatenate([jnp.concatenate([dup_prev[g][1], dup_cur[g][1]], axis=0), ones], axis=1)
        oas.append(_mm(e, v_aug))
    cols = [[] for _ in blocks]
    for (b, g), oa, m_g in zip(probs, oas, ms):
        outs = []
        for j in range(ATT_GROUP):
            oj = oa[j * BLOCK_Q:(j + 1) * BLOCK_Q]
            den = oj[:, LANES:] + jnp.exp2(sinks[g * ATT_GROUP + j] - m_g[j])
            outs.append(oj[:, :LANES] / den)
        for pr in range(ATT_GROUP // 2):
            cols[b].append(jnp.where(lane < HEAD_DIM, outs[2 * pr], outs[2 * pr + 1]))
    return [jnp.concatenate(cb, axis=1) for cb in cols]


def _split3(x):
    hi = x.astype(BF16)
    r1 = x - hi.astype(F32)
    mid = r1.astype(BF16)
    lo = (r1 - mid.astype(F32)).astype(BF16)
    return hi, mid, lo


def _mm_const_lhs(c, x):
    hi, mid, lo = _split3(x)
    return _mm(c, hi) + _mm(c, mid) + _mm(c, lo)


def _bd(x, bd_mask):
    return jnp.where(bd_mask, jnp.concatenate([x] * DN_HEADS, axis=0), jnp.zeros((), x.dtype))


def _head_transpose(x):
    xt = x.T
    return jnp.concatenate([xt[hd * DN_CHUNK:(hd + 1) * DN_CHUNK] for hd in range(DN_HEADS)], axis=1)


def _dn_body(u_ref, cw_ref, alog_ref, dtb_ref, nrm_ref, o_ref,
             tail_ref, s_ref, wu_ref, qeff_ref, glast_ref, z_ref, *, tiles_per_seq):
    hw = DN_HEADS * HEAD_DIM
    rows = o_ref.shape[0]
    c = DN_CHUNK
    ch = range(rows // c)
    sl = [slice(i * c, (i + 1) * c) for i in ch]
    step = pl.program_id(0)

    @pl.when(step == 0)
    def _():
        for ref in (tail_ref, s_ref, wu_ref, qeff_ref, glast_ref, z_ref):
            ref[...] = jnp.zeros_like(ref)

    fresh = step % tiles_per_seq == 0
    prev_fresh = (step + tiles_per_seq - 1) % tiles_per_seq == 0

    r_i = lax.broadcasted_iota(jnp.int32, (c, hw), 0)
    l_j = lax.broadcasted_iota(jnp.int32, (c, hw), 1) % c
    diag = r_i == l_j
    low_incl = r_i >= l_j
    low_strict = r_i > l_j
    bd_r = lax.broadcasted_iota(jnp.int32, (hw, hw), 0) // c
    bd_c = lax.broadcasted_iota(jnp.int32, (hw, hw), 1) // c
    bd_mask = bd_r == bd_c
    ones_bd = jnp.where(bd_mask, 1.0, 0.0).astype(BF16)
    t_r = lax.broadcasted_iota(jnp.int32, (DN_CUMSUM_ROWS, DN_CUMSUM_ROWS), 0)
    t_c = lax.broadcasted_iota(jnp.int32, (DN_CUMSUM_ROWS, DN_CUMSUM_ROWS), 1)
    chunk_tril = jnp.where((t_r // c == t_c // c) & (t_r >= t_c), 1.0, 0.0).astype(BF16)

    def group_sum(x):
        return _mm(x.astype(BF16), ones_bd)

    def pmm(a_pk, b_pk):
        return _mm(a_pk.astype(BF16), _bd(b_pk.astype(BF16), bd_mask))

    prev_wu = [wu_ref[i] for i in ch]
    prev_qeff = [qeff_ref[i] for i in ch]
    prev_glast = [glast_ref[i] for i in ch]
    carry = {"state": jnp.where(prev_fresh, 0.0, s_ref[...]), "outs": []}

    def recur(i):
        res = pmm(jnp.concatenate([prev_wu[i][:c, :hw], prev_qeff[i]], axis=0), carry["state"])
        carry["outs"].append(res[c:] + prev_wu[i][c:, hw:])
        carry["state"] = carry["state"] * jnp.exp(prev_glast[i]) - res[:c] + prev_wu[i][:c, hw:]

    x = u_ref[:, :3 * hw]
    qkv = _conv_rows(x, jnp.where(fresh, 0.0, tail_ref[...]), cw_ref[...])
    tail_ref[...] = x[rows - SUBLANES:]
    qkv = qkv * jax.nn.sigmoid(qkv)
    q, k, v = qkv[:, :hw], qkv[:, hw:2 * hw], qkv[:, 2 * hw:]
    beta = jax.nn.sigmoid(u_ref[:, 4 * hw:5 * hw])
    a = u_ref[:, 5 * hw:6 * hw] + dtb_ref[...]
    g = -jnp.exp(alog_ref[...]) * (jnp.maximum(a, 0.0) + jnp.log(1.0 + jnp.exp(-jnp.abs(a))))
    q = q * (lax.rsqrt(group_sum(q * q) + EPS) * (HEAD_DIM ** -0.5))
    k = k * lax.rsqrt(group_sum(k * k) + EPS)
    gcum = jnp.concatenate(
        [_mm_const_lhs(chunk_tril, g[r0:r0 + DN_CUMSUM_ROWS]) for r0 in range(0, rows, DN_CUMSUM_ROWS)], axis=0)
    recur(0)
    qc, kc, gc = [q[s] for s in sl], [k[s] for s in sl], [gcum[s] for s in sl]
    g_last = [gc[i][c - 1:c] for i in ch]
    eg = [jnp.exp(gc[i]) for i in ch]
    k_beta = [kc[i] * beta[sl[i]] for i in ch]
    v_beta = [v[sl[i]] * beta[sl[i]] for i in ch]
    g_row = [jnp.sum(jnp.where(diag, gc[i], 0.0), axis=0, keepdims=True) for i in ch]
    decay = [jnp.exp(jnp.where(low_incl, gc[i] - g_row[i], -jnp.inf)) for i in ch]
    kq = [_mm_nt(jnp.concatenate([k_beta[i], qc[i]], axis=0).astype(BF16), _bd(kc[i].astype(BF16), bd_mask))
          for i in ch]
    attn = [kq[i][c:] * decay[i] for i in ch]
    recur(1)
    m1 = [-jnp.where(low_strict, kq[i][:c] * decay[i], 0.0) for i in ch]
    m2 = [pmm(m1[i], m1[i]) for i in ch]
    recur(2)
    m34 = [pmm(jnp.concatenate([m1[i], m2[i]], axis=0), m2[i]) for i in ch]
    psum = [jnp.where(diag, 1.0, 0.0) + m1[i] + m2[i] + m34[i][:c] for i in ch]
    mp = [m34[i][c:] for i in ch]
    recur(3)
    for lvl in range(3):
        both = [pmm(jnp.concatenate([mp[i], psum[i]], axis=0), mp[i]) for i in ch]
        psum = [psum[i] + both[i][c:] for i in ch]
        mp = [both[i][:c] for i in ch]
        recur(4 + lvl)
    tb = [(psum[i] + pmm(psum[i], mp[i])).astype(BF16) for i in ch]
    u_c = [_mm(tb[i], _bd(v_beta[i].astype(BF16), bd_mask)).astype(BF16) for i in ch]
    w_c = [_mm(tb[i], _bd((k_beta[i] * eg[i]).astype(BF16), bd_mask)).astype(BF16) for i in ch]
    for i in range(7, len(ch)):
        recur(i)
    s_ref[...] = carry["state"]
    lhs = [jnp.concatenate([_head_transpose(kc[i] * jnp.exp(g_last[i] - gc[i])), attn[i]], axis=0).astype(BF16)
           for i in ch]
    wu = [_mm(lhs[i], jnp.concatenate([_bd(w_c[i], bd_mask), _bd(u_c[i], bd_mask)], axis=1)) for i in ch]

    o = jnp.concatenate(carry["outs"], axis=0)
    z = z_ref[...]
    ms = group_sum(o * o) * (1.0 / HEAD_DIM)
    o_ref[...] = (o * lax.rsqrt(ms + EPS) * nrm_ref[...] * (z * jax.nn.sigmoid(z))).astype(o_ref.dtype)

    for i in ch:
        wu_ref[i] = wu[i]
        qeff_ref[i] = qc[i] * eg[i] - wu[i][c:, :hw]
        glast_ref[i] = g_last[i]
    z_ref[...] = u_ref[:, 3 * hw:4 * hw]


def _deltanet(u, conv_w, a_log, dt_bias, norm, tiles_per_seq):
    n, uw = u.shape
    hw = DN_HEADS * HEAD_DIM
    n_tiles = n // DN_TILE
    n_ch = DN_TILE // DN_CHUNK
    return pl.pallas_call(
        functools.partial(_dn_body, tiles_per_seq=tiles_per_seq),
        grid=(n_tiles + 1,),
        in_specs=[pl.BlockSpec((DN_TILE, uw), lambda i: (jnp.minimum(i, n_tiles - 1), 0))]
        + [_const_spec(a.shape) for a in (conv_w, a_log, dt_bias, norm)],
        out_specs=pl.BlockSpec((DN_TILE, hw), lambda i: (jnp.maximum(i - 1, 0), 0)),
        out_shape=jax.ShapeDtypeStruct((n, hw), BF16),
        scratch_shapes=[pltpu.VMEM((SUBLANES, 3 * hw), F32), pltpu.VMEM((DN_CHUNK, hw), F32),
                        pltpu.VMEM((n_ch, 2 * DN_CHUNK, 2 * hw), F32), pltpu.VMEM((n_ch, DN_CHUNK, hw), F32),
                        pltpu.VMEM((n_ch, 1, hw), F32), pltpu.VMEM((DN_TILE, hw), F32)],
        compiler_params=_cparams("arbitrary"),
        name="deltanet",
    )(u, conv_w, a_log, dt_bias, norm)


def _rel_bucket_table():
    qi = jnp.arange(BLOCK_Q)[:, None]
    kj = jnp.arange(2 * BLOCK_Q)[None, :]
    dist = BLOCK_Q + qi - kj
    band = (dist >= 0) & (dist < BLOCK_Q)
    d = jnp.maximum(dist, 0)
    max_exact = REL_BUCKETS // 2
    large = max_exact + (jnp.log(jnp.maximum(d, 1).astype(F32) / max_exact)
                         / math.log(REL_MAX_DIST / max_exact)
                         * (REL_BUCKETS - max_exact)).astype(jnp.int32)
    large = jnp.minimum(large, REL_BUCKETS - 1)
    return jnp.where(band, jnp.where(d < max_exact, d, large), -1).astype(jnp.int32)


def _block_diag(w):
    hh, n, _ = w.shape
    eye = jnp.eye(hh, dtype=w.dtype)
    return (eye[:, None, :, None] * w[:, :, None, :]).reshape(hh * n, hh * n)


def kernel(x, p, ffn1_norm, ffn1_w_gate, ffn1_w_up, ffn1_w_down, mix_norm, w_in, lru_conv_w, lru_conv_b, lru_w_a, lru_b_a, lru_w_x, lru_b_x, lru_lambda, attn_sinks, rel_bias, dn_conv_w, dn_a_log, dn_dt_bias, dn_norm, w_out, ffn2_norm, ffn2_w_gate, ffn2_w_up, ffn2_w_down, ple_norm, ple_w_gate, ple_w_proj, final_norm):
    bsz, seq, d = x.shape
    depth = w_in.shape[0]
    n = bsz * seq
    lru_w = lru_lambda.shape[1]
    q_w = attn_sinks.shape[1] * HEAD_DIM
    dn_w = DN_HEADS * HEAD_DIM
    kv_w = w_in.shape[2] - 2 * lru_w - q_w - 4 * dn_w - 2 * DN_HEADS
    q0 = 2 * lru_w
    main = q0 + q_w + kv_w + 4 * dn_w
    groups = ((2 * lru_w, F32), (q_w, BF16), (kv_w, F32), (6 * dn_w, F32))
    bucket = _rel_bucket_table()
    row = lambda v: v.reshape(1, -1).astype(F32)
    rep = lambda v: jnp.repeat(v, HEAD_DIM).reshape(1, -1)
    bf = lambda w: w.astype(BF16)
    ffn1 = (bf(ffn1_w_gate), bf(ffn1_w_up), bf(ffn1_w_down))
    ffn2 = (bf(ffn2_w_gate), bf(ffn2_w_up), bf(ffn2_w_down))
    wo, wpg, wpp = bf(w_out), bf(ple_w_gate), bf(ple_w_proj)
    p_rows = p.reshape(depth, n, -1)

    h = x.reshape(n, d)
    for l in range(depth):
        w_aug = bf(jnp.concatenate(
            [w_in[l, :, :q0],
             w_in[l, :, q0:q0 + q_w] * (HEAD_DIM ** -0.5 * LOG2E),
             w_in[l, :, q0 + q_w:main],
             jnp.repeat(w_in[l, :, main:main + DN_HEADS], HEAD_DIM, axis=1),
             jnp.repeat(w_in[l, :, main + DN_HEADS:], HEAD_DIM, axis=1)], axis=1))
        h, u_lru, u_q, u_kv, u_dn = _pre(h, l, row(ffn1_norm[l]), *ffn1, row(mix_norm[l]), w_aug, groups)

        w_ax = bf(jnp.concatenate([_block_diag(lru_w_a[l]), _block_diag(lru_w_x[l])], axis=1))
        b_ax = jnp.concatenate([lru_b_a[l], lru_b_x[l]]).reshape(1, -1)
        lru_consts = (lru_conv_w[l], row(lru_conv_b[l]), w_ax, b_ax, row(lru_lambda[l]))
        y_dn = _deltanet(u_dn, dn_conv_w[l], rep(dn_a_log[l]), rep(dn_dt_bias[l]),
                         jnp.tile(dn_norm[l], DN_HEADS).reshape(1, -1), seq // DN_TILE)

        h = _post(h, u_lru, u_q, u_kv, y_dn, p_rows, l, bucket, rel_bias, attn_sinks[l], lru_consts,
                  wo, row(ffn2_norm[l]), *ffn2, row(ple_norm[l]), wpg, wpp, row(final_norm),
                  final=(l == depth - 1), tiles_per_seq=seq // TOKEN_TILE)
    return h.reshape(bsz, seq, d)
```
